```python
import jax, jax.numpy as jnp
from jax import lax
import numpy as np

D_MODEL = 2048
BATCH = 32
SEQ = 256
DEPTH = 4
DEC_BATCH = 8
DEC_SEQ = 1024
PAST_LEN = 256

GRID_W = 64
HEAD_DIM = 128
H_A = 6
H_B = 6
KV_B = 2
REP_B = H_B // KV_B
G_C = 4
C_GROUP = 128
W_A = H_A * HEAD_DIM
W_B = H_B * HEAD_DIM
W_C = G_C * C_GROUP
MIX_WIDTH = W_A + W_B + W_C
NA_ROWS = 8
NA_COLS = 16
WIN = 128
BLK = 128
CHUNK = 128
D_FF = -(-8 * D_MODEL // (3 * 256)) * 256
N_MOD = 6
ROPE_BASE = 10000.0
NORM_EPS = 1e-6
ATTN_SCALE = HEAD_DIM ** -0.5
Q_A = 0
K_A = Q_A + W_A
V_A = K_A + W_A
Q_B = V_A + W_A
K_B = Q_B + W_B
V_B = K_B + KV_B * HEAD_DIM
U_C = V_B + KV_B * HEAD_DIM
V_C = U_C + W_C
IN_WIDTH = V_C + W_C

kernel_name = 'hybrid_prefix_diffusion_step'


def rms_norm(x, g):
    x32 = x.astype(jnp.float32)
    y = x32 * lax.rsqrt(jnp.mean(x32 * x32, axis=-1, keepdims=True) + NORM_EPS)
    return (y * g.astype(jnp.float32)).astype(x.dtype)


def layer_norm(x, g, b):
    x32 = x.astype(jnp.float32)
    mu = jnp.mean(x32, axis=-1, keepdims=True)
    xc = x32 - mu
    y = xc * lax.rsqrt(jnp.mean(xc * xc, axis=-1, keepdims=True) + NORM_EPS)
    return (y * g.astype(jnp.float32) + b.astype(jnp.float32)).astype(x.dtype)


def rope_1d(x, pos):
    half = x.shape[-1] // 2
    freqs = ROPE_BASE ** (-jnp.arange(half, dtype=jnp.float32) / half)
    ang = pos.astype(jnp.float32)[:, None] * freqs[None, :]
    cos, sin = jnp.cos(ang), jnp.sin(ang)
    x32 = x.astype(jnp.float32)
    x1, x2 = x32[..., :half], x32[..., half:]
    return jnp.concatenate([x1 * cos - x2 * sin, x1 * sin + x2 * cos], axis=-1).astype(x.dtype)


def axial_rope(x):
    n = x.shape[-2]
    t = jnp.arange(n)
    half = x.shape[-1] // 2
    return jnp.concatenate([rope_1d(x[..., :half], t // GRID_W),
                            rope_1d(x[..., half:], t % GRID_W)], axis=-1)


def modulation(cond, w_mod, b_mod):
    m = jax.nn.silu(cond) @ w_mod + b_mod
    return jnp.split(m[..., None, :], N_MOD, axis=-1)


def split_projection(h, w_in):
    z = h @ w_in
    b, L, _ = z.shape
    def heads(a, nh):
        return a.reshape(b, L, nh, HEAD_DIM).transpose(0, 2, 1, 3)
    q_a = heads(z[..., Q_A:K_A], H_A)
    k_a = heads(z[..., K_A:V_A], H_A)
    v_a = heads(z[..., V_A:Q_B], H_A)
    q_b = z[..., Q_B:K_B].reshape(b, L, KV_B, REP_B, HEAD_DIM).transpose(0, 2, 3, 1, 4)
    k_b = heads(z[..., K_B:V_B], KV_B)
    v_b = heads(z[..., V_B:U_C], KV_B)
    u_c = jax.nn.gelu(z[..., U_C:V_C])
    v_c = jax.nn.gelu(z[..., V_C:IN_WIDTH])
    return q_a, k_a, v_a, q_b, k_b, v_b, u_c, v_c


def merge_heads(o_a, o_b, o_c, w_out):
    b, _, L, _ = o_a.shape
    o_a = o_a.transpose(0, 2, 1, 3).reshape(b, L, W_A)
    o_b = o_b.transpose(0, 3, 1, 2, 4).reshape(b, L, W_B)
    return jnp.concatenate([o_a, o_b, o_c], axis=-1) @ w_out


def context_attention(q, k, v, sink):
    b, g, r, s, d = q.shape
    nq = s // BLK
    qb = jnp.moveaxis(q.reshape(b, g, r, nq, BLK, d), 3, 0)
    def block(qi):
        sc = jnp.einsum('bgrqd,bgkd->bgrqk', qi, k).astype(jnp.float32) * ATTN_SCALE
        if sink is not None:
            sk = jnp.broadcast_to(sink.astype(jnp.float32)[None, :, :, None, None], (b, g, r, BLK, 1))
            sc = jnp.concatenate([sc, sk], axis=-1)
        p = jax.nn.softmax(sc, axis=-1)[..., :s].astype(v.dtype)
        return jnp.einsum('bgrqk,bgkd->bgrqd', p, v)
    out = lax.map(block, qb)
    return jnp.moveaxis(out, 0, 3).reshape(b, g, r, s, d)


def neighbourhood_attention(q, k, v, ck, cv, rpb):
    b, h, n, d = q.shape
    rows = n // GRID_W
    kr = min(NA_ROWS, rows)
    r = jnp.arange(rows)
    row_idx = jnp.clip(r - kr // 2, 0, rows - kr)[:, None] + jnp.arange(kr)[None, :]
    col = jnp.arange(GRID_W)
    c0 = jnp.clip(col - NA_COLS // 2, 0, GRID_W - NA_COLS)
    col_ok = (col[None, :] >= c0[:, None]) & (col[None, :] < c0[:, None] + NA_COLS)
    idx_r = row_idx - r[:, None] + NA_ROWS - 1
    idx_c = jnp.clip(col[None, :] - col[:, None] + NA_COLS - 1, 0, 2 * NA_COLS - 2)
    bias = rpb[:, idx_r[:, None, :, None], idx_c[None, :, None, :]].astype(jnp.float32)
    q5 = q.reshape(b, h, rows, GRID_W, d)
    k_band = k.reshape(b, h, rows, GRID_W, d)[:, :, row_idx]
    v_band = v.reshape(b, h, rows, GRID_W, d)[:, :, row_idx]
    s_loc = jnp.einsum('bhrqd,bhrjkd->bhrqjk', q5, k_band).astype(jnp.float32) * ATTN_SCALE + bias[None]
    s_loc = jnp.where(col_ok[:, None, :], s_loc, -jnp.inf).reshape(b, h, rows, GRID_W, kr * GRID_W)
    s_ctx = jnp.einsum('bhrqd,bhsd->bhrqs', q5, ck).astype(jnp.float32) * ATTN_SCALE
    p = jax.nn.softmax(jnp.concatenate([s_loc, s_ctx], axis=-1), axis=-1).astype(v.dtype)
    p_loc = p[..., :kr * GRID_W].reshape(b, h, rows, GRID_W, kr, GRID_W)
    p_ctx = p[..., kr * GRID_W:]
    out = (jnp.einsum('bhrqjk,bhrjkd->bhrqd', p_loc, v_band)
           + jnp.einsum('bhrqs,bhsd->bhrqd', p_ctx, cv))
    return out.reshape(b, h, n, d)


def window_attention(q, k, v, ck, cv, sink):
    b, g, r, n, d = q.shape
    nb = n // BLK
    span = BLK + 2 * WIN
    pad = ((0, 0), (0, 0), (WIN, WIN), (0, 0))
    kp, vp = jnp.pad(k, pad), jnp.pad(v, pad)
    idx = (jnp.arange(nb) * BLK)[:, None] + jnp.arange(span)[None, :]
    kb, vb = kp[:, :, idx], vp[:, :, idx]
    qb = q.reshape(b, g, r, nb, BLK, d)
    qpos = jnp.arange(n).reshape(nb, BLK)
    kpos = idx - WIN
    ok = ((jnp.abs(qpos[:, :, None] - kpos[:, None, :]) <= WIN)
          & (kpos[:, None, :] >= 0) & (kpos[:, None, :] < n))
    s_loc = jnp.einsum('bgrnqd,bgnkd->bgrnqk', qb, kb).astype(jnp.float32) * ATTN_SCALE
    s_loc = jnp.where(ok, s_loc, -jnp.inf)
    s_ctx = jnp.einsum('bgrnqd,bgsd->bgrnqs', qb, ck).astype(jnp.float32) * ATTN_SCALE
    sk = jnp.broadcast_to(sink.astype(jnp.float32)[None, :, :, None, None, None], (b, g, r, nb, BLK, 1))
    p = jax.nn.softmax(jnp.concatenate([s_loc, s_ctx, sk], axis=-1), axis=-1).astype(v.dtype)
    sc = ck.shape[2]
    out = (jnp.einsum('bgrnqk,bgnkd->bgrnqd', p[..., :span], vb)
           + jnp.einsum('bgrnqs,bgsd->bgrnqd', p[..., span:span + sc], cv))
    return out.reshape(b, g, r, n, d)


def chunk_gmlp(u, v, w_s, b_s, ln_g, ln_b):
    b, L, _ = v.shape
    vn = layer_norm(v, ln_g, ln_b).reshape(b, L // CHUNK, CHUNK, G_C, C_GROUP)
    s = jnp.einsum('gpq,bnqgc->bnpgc', w_s, vn) + b_s.T[None, None, :, :, None]
    return u * s.reshape(b, L, W_C)


def ffn_sublayer(x, shift, scale, gate, g_pre, g_post, w_gate, w_up, w_down):
    h = rms_norm(x, g_pre) * (1 + scale) + shift
    f = (jax.nn.silu(h @ w_gate) * (h @ w_up)) @ w_down
    return x + gate * rms_norm(f, g_post)


def setup_inputs(seed: int = 0) -> dict:
    key = jax.random.key(seed)
    ks = jax.random.split(key, 26)
    def nrm(k, shape, s):
        return jax.random.normal(k, shape, jnp.float32) * s
    D = D_MODEL
    return {
        'x_prompt': nrm(ks[0], (BATCH, SEQ, D), 1.0),
        'x_sample': nrm(ks[1], (DEC_BATCH, DEC_SEQ, D), 1.0),
        'cache_a_k': nrm(ks[2], (DEC_BATCH, DEPTH, H_A, PAST_LEN, HEAD_DIM), 1.0),
        'cache_a_v': nrm(ks[3], (DEC_BATCH, DEPTH, H_A, PAST_LEN, HEAD_DIM), 1.0),
        'cache_b_k': nrm(ks[4], (DEC_BATCH, DEPTH, KV_B, PAST_LEN, HEAD_DIM), 1.0),
        'cache_b_v': nrm(ks[5], (DEC_BATCH, DEPTH, KV_B, PAST_LEN, HEAD_DIM), 1.0),
        'c': nrm(ks[6], (DEC_BATCH, D), 1.0),
        'c_ctx': nrm(ks[7], (D,), 1.0),
        'mod_w': nrm(ks[8], (DEPTH, D, N_MOD * D), 0.5 * D ** -0.5),
        'mod_b': nrm(ks[9], (DEPTH, N_MOD * D), 0.02),
        'norm_mix_pre': 1.0 + nrm(ks[10], (DEPTH, D), 0.05),
        'norm_mix_post': 1.0 + nrm(ks[11], (DEPTH, D), 0.05),
        'norm_ffn_pre': 1.0 + nrm(ks[12], (DEPTH, D), 0.05),
        'norm_ffn_post': 1.0 + nrm(ks[13], (DEPTH, D), 0.05),
        'w_in': nrm(ks[14], (DEPTH, D, IN_WIDTH), D ** -0.5),
        'w_out': nrm(ks[15], (DEPTH, MIX_WIDTH, D), MIX_WIDTH ** -0.5),
        'rpb_a': nrm(ks[16], (DEPTH, H_A, 2 * NA_ROWS - 1, 2 * NA_COLS - 1), 0.3),
        'sink_b': nrm(ks[17], (DEPTH, KV_B, REP_B), 1.0),
        'gmlp_ln_g': 1.0 + nrm(ks[18], (DEPTH, W_C), 0.05),
        'gmlp_ln_b': nrm(ks[19], (DEPTH, W_C), 0.02),
        'gmlp_w': nrm(ks[20], (DEPTH, G_C, CHUNK, CHUNK), CHUNK ** -0.5),
        'gmlp_b': 1.0 + nrm(ks[21], (DEPTH, G_C, CHUNK), 0.1),
        'w_gate': nrm(ks[22], (DEPTH, D, D_FF), D ** -0.5),
        'w_up': nrm(ks[23], (DEPTH, D, D_FF), D ** -0.5),
        'w_down': nrm(ks[24], (DEPTH, D_FF, D), D_FF ** -0.5),
    }


def reference(x_prompt, x_sample, cache_a_k, cache_a_v, cache_b_k, cache_b_v, c, c_ctx,
              mod_w, mod_b, norm_mix_pre, norm_mix_post, norm_ffn_pre, norm_ffn_post,
              w_in, w_out, rpb_a, sink_b, gmlp_ln_g, gmlp_ln_b, gmlp_w, gmlp_b,
              w_gate, w_up, w_down):
    xp = x_prompt
    ka_list, va_list, kb_list, vb_list = [], [], [], []
    for l in range(DEPTH):
        sm, cm, gm, sf, cf, gf = modulation(c_ctx, mod_w[l], mod_b[l])
        h = rms_norm(xp, norm_mix_pre[l]) * (1 + cm) + sm
        q_a, k_a, v_a, q_b, k_b, v_b, u_c, v_c = split_projection(h, w_in[l])
        o_a = context_attention(q_a[:, :, None], k_a, v_a, None)[:, :, 0]
        o_b = context_attention(q_b, k_b, v_b, sink_b[l])
        o_c = chunk_gmlp(u_c, v_c, gmlp_w[l], gmlp_b[l], gmlp_ln_g[l], gmlp_ln_b[l])
        xp = xp + gm * rms_norm(merge_heads(o_a, o_b, o_c, w_out[l]), norm_mix_post[l])
        xp = ffn_sublayer(xp, sf, cf, gf, norm_ffn_pre[l], norm_ffn_post[l], w_gate[l], w_up[l], w_down[l])
        ka_list.append(k_a)
        va_list.append(v_a)
        kb_list.append(k_b)
        vb_list.append(v_b)
    xs = x_sample
    for l in range(DEPTH):
        sm, cm, gm, sf, cf, gf = modulation(c, mod_w[l], mod_b[l])
        h = rms_norm(xs, norm_mix_pre[l]) * (1 + cm) + sm
        q_a, k_a, v_a, q_b, k_b, v_b, u_c, v_c = split_projection(h, w_in[l])
        o_a = neighbourhood_attention(q_a, k_a, v_a, cache_a_k[:, l], cache_a_v[:, l], rpb_a[l])
        o_b = window_attention(axial_rope(q_b), axial_rope(k_b), v_b,
                               cache_b_k[:, l], cache_b_v[:, l], sink_b[l])
        o_c = chunk_gmlp(u_c, v_c, gmlp_w[l], gmlp_b[l], gmlp_ln_g[l], gmlp_ln_b[l])
        xs = xs + gm * rms_norm(merge_heads(o_a, o_b, o_c, w_out[l]), norm_mix_post[l])
        xs = ffn_sublayer(xs, sf, cf, gf, norm_ffn_pre[l], norm_ffn_post[l], w_gate[l], w_up[l], w_down[l])
    new_a_k = jnp.stack(ka_list, axis=1)
    new_a_v = jnp.stack(va_list, axis=1)
    new_b_k = jnp.stack(kb_list, axis=1)
    new_b_v = jnp.stack(vb_list, axis=1)
    return (xp, xs, new_a_k, new_a_v, new_b_k, new_b_v)
```

```python
import functools

import jax
import jax.numpy as jnp
import numpy as np
from jax import lax
from jax.experimental import pallas as pl
from jax.experimental.pallas import tpu as pltpu

D_MODEL = 2048
BATCH = 32
SEQ = 256
DEPTH = 4
DEC_BATCH = 8
DEC_SEQ = 1024
PAST_LEN = 256
GRID_W = 64
GRID_ROWS = DEC_SEQ // GRID_W
HEAD_DIM = 128
H_A = 6
H_B = 6
KV_B = 2
REP_B = H_B // KV_B
G_C = 4
C_GROUP = 128
W_A = H_A * HEAD_DIM
W_B = H_B * HEAD_DIM
W_C = G_C * C_GROUP
MIX_WIDTH = W_A + W_B + W_C
NA_ROWS = 8
NA_COLS = 16
WIN = 128
BLK = 128
CHUNK = 128
D_FF = 5632
N_MOD = 6
ROPE_BASE = 10000.0
NORM_EPS = 1e-6
ATTN_SCALE = HEAD_DIM ** -0.5
Q_A = 0
K_A = Q_A + W_A
V_A = K_A + W_A
Q_B = V_A + W_A
K_B = Q_B + W_B
V_B = K_B + KV_B * HEAD_DIM
U_C = V_B + KV_B * HEAD_DIM
V_C = U_C + W_C
IN_WIDTH = V_C + W_C

N_CTX_TOK = BATCH * SEQ
N_LAT_TOK = DEC_BATCH * DEC_SEQ
MOD_ROWS = 16
CTX_MOD_ROW = DEC_BATCH

VMEM_LIMIT_BYTES = 56 * 1024 * 1024

BF16 = jnp.bfloat16
F32 = jnp.float32


def _params(*sem):
    return pltpu.CompilerParams(dimension_semantics=sem, vmem_limit_bytes=VMEM_LIMIT_BYTES)


def _dot(a, b):
    return jnp.dot(a, b, preferred_element_type=F32)


def _dot_nt(a, b):
    return lax.dot_general(a, b, (((1,), (1,)), ((), ())), preferred_element_type=F32)


def _rms(x, g):
    return (x * lax.rsqrt(jnp.mean(x * x, axis=-1, keepdims=True) + NORM_EPS)) * g


def _mod_kernel(c_ref, w_ref, b_ref, o_ref):
    s = jax.nn.silu(c_ref[...])
    o_ref[...] = _dot(s.astype(BF16), w_ref[...].astype(BF16)) + b_ref[...]


def _modulation(cond, mod_w, mod_b):
    tn = 1024
    n = N_MOD * D_MODEL
    return pl.pallas_call(
        _mod_kernel,
        grid=(DEPTH, n // tn),
        in_specs=[
            pl.BlockSpec((MOD_ROWS, D_MODEL), lambda l, j: (0, 0)),
            pl.BlockSpec((None, D_MODEL, tn), lambda l, j: (l, 0, j)),
            pl.BlockSpec((None, 1, tn), lambda l, j: (l, 0, j)),
        ],
        out_specs=pl.BlockSpec((None, MOD_ROWS, tn), lambda l, j: (l, 0, j)),
        out_shape=jax.ShapeDtypeStruct((DEPTH, MOD_ROWS, n), F32),
        compiler_params=_params("parallel", "parallel"),
        name="modulation",
    )(cond, mod_w, mod_b.reshape(DEPTH, 1, n))


def _mod_spec(l, which, row_fn):
    return pl.BlockSpec((None, None, None, 1, D_MODEL),
                        lambda i, *_: (l, row_fn(i), which, 0, 0))


def _vec_spec(l):
    return pl.BlockSpec((None, 1, D_MODEL), lambda i, *_: (l, 0, 0))


def _in_kernel(x_ref, g_ref, sc_ref, sh_ref, w_ref, z_ref, h_ref):
    @pl.when(pl.program_id(1) == 0)
    def _():
        h = _rms(x_ref[...], g_ref[...]) * (1.0 + sc_ref[...]) + sh_ref[...]
        h_ref[...] = h.astype(BF16)

    z_ref[...] = _dot(h_ref[...], w_ref[...])


def _in_proj(x, mod5, g_pre, w_in, l, row_fn, tm):
    m = x.shape[0]
    tn = 1152
    return pl.pallas_call(
        _in_kernel,
        grid=(m // tm, IN_WIDTH // tn),
        in_specs=[
            pl.BlockSpec((tm, D_MODEL), lambda i, j: (i, 0)),
            _vec_spec(l),
            _mod_spec(l, 1, row_fn),
            _mod_spec(l, 0, row_fn),
            pl.BlockSpec((None, D_MODEL, tn), lambda i, j: (l, 0, j)),
        ],
        out_specs=pl.BlockSpec((tm, tn), lambda i, j: (i, j)),
        out_shape=jax.ShapeDtypeStruct((m, IN_WIDTH), F32),
        scratch_shapes=[pltpu.VMEM((tm, D_MODEL), BF16)],
        compiler_params=_params("parallel", "arbitrary"),
        name="in_proj",
    )(x, g_pre, mod5, mod5, w_in)


def _softmax_pv(scores, values, extra_logit=None):
    m = scores[0].max(axis=-1, keepdims=True)
    for s in scores[1:]:
        m = jnp.maximum(m, s.max(axis=-1, keepdims=True))
    if extra_logit is not None:
        m = jnp.maximum(m, extra_logit)
    den = None
    acc = None
    for s, v in zip(scores, values):
        e = jnp.exp(s - m)
        d = e.sum(axis=-1, keepdims=True)
        a = _dot(e.astype(BF16), v)
        den = d if den is None else den + d
        acc = a if acc is None else acc + a
    if extra_logit is not None:
        den = den + jnp.exp(extra_logit - m)
    return acc / den


def _sink_column(sink_ref, base, rows_per_head):
    rows = lax.broadcasted_iota(jnp.int32, (REP_B * rows_per_head, 1), 0)
    col = jnp.full((REP_B * rows_per_head, 1), sink_ref[base], F32)
    for r in range(1, REP_B):
        col = jnp.where(rows >= r * rows_per_head, sink_ref[base + r], col)
    return col


def _gmlp(u_raw, v_raw, gw_ref, gb_ref, lng_ref, lnb_ref, write):
    u = jax.nn.gelu(u_raw)
    v = jax.nn.gelu(v_raw)
    mu = jnp.mean(v, axis=-1, keepdims=True)
    xc = v - mu
    vn = xc * lax.rsqrt(jnp.mean(xc * xc, axis=-1, keepdims=True) + NORM_EPS)
    vn = (vn * lng_ref[...] + lnb_ref[...]).astype(BF16)
    for n in range(u_raw.shape[0] // CHUNK):
        rs = slice(n * CHUNK, (n + 1) * CHUNK)
        for g in range(G_C):
            cs = slice(g * C_GROUP, (g + 1) * C_GROUP)
            s = _dot(gw_ref[g].astype(BF16), vn[rs, cs]) + gb_ref[g]
            write(n, g, u[rs, cs] * s)


def _ctx_mix_kernel(l, sink_ref, z_ref, gw_ref, gb_ref, lng_ref, lnb_ref,
                    o_ref, ka_ref, va_ref, kb_ref, vb_ref):
    def col(off, i=0):
        return z_ref[:, off + i * HEAD_DIM: off + (i + 1) * HEAD_DIM]

    for h in range(H_A):
        k = col(K_A, h)
        v = col(V_A, h)
        ka_ref[h] = k
        va_ref[h] = v
        s = _dot_nt(col(Q_A, h).astype(BF16), k.astype(BF16)) * ATTN_SCALE
        o_ref[:, h * HEAD_DIM:(h + 1) * HEAD_DIM] = _softmax_pv([s], [v.astype(BF16)]).astype(BF16)

    for g in range(KV_B):
        k = col(K_B, g)
        v = col(V_B, g)
        kb_ref[g] = k
        vb_ref[g] = v
        q = jnp.concatenate([col(Q_B, g * REP_B + r) for r in range(REP_B)], axis=0).astype(BF16)
        s = _dot_nt(q, k.astype(BF16)) * ATTN_SCALE
        sink = _sink_column(sink_ref, l * H_B + g * REP_B, SEQ)
        o = _softmax_pv([s], [v.astype(BF16)], sink).astype(BF16)
        for r in range(REP_B):
            c0 = W_A + (g * REP_B + r) * HEAD_DIM
            o_ref[:, c0:c0 + HEAD_DIM] = o[r * SEQ:(r + 1) * SEQ]

    def write(n, g, val):
        c0 = W_A + W_B + g * C_GROUP
        o_ref[n * CHUNK:(n + 1) * CHUNK, c0:c0 + C_GROUP] = val.astype(BF16)

    _gmlp(z_ref[:, U_C:V_C], z_ref[:, V_C:IN_WIDTH], gw_ref, gb_ref, lng_ref, lnb_ref, write)


def _gmlp_specs(l):
    return [
        pl.BlockSpec((None, G_C, CHUNK, CHUNK), lambda i, *_: (l, 0, 0, 0)),
        pl.BlockSpec((None, G_C, CHUNK, 1), lambda i, *_: (l, 0, 0, 0)),
        pl.BlockSpec((None, 1, W_C), lambda i, *_: (l, 0, 0)),
        pl.BlockSpec((None, 1, W_C), lambda i, *_: (l, 0, 0)),
    ]


def _ctx_mix(z, sink, gmlp_w, gmlp_b4, ln_g, ln_b, l):
    kv = lambda nh: jax.ShapeDtypeStruct((BATCH, nh, SEQ, HEAD_DIM), F32)
    kv_spec = lambda nh: pl.BlockSpec((None, nh, SEQ, HEAD_DIM), lambda b: (b, 0, 0, 0))
    return pl.pallas_call(
        functools.partial(_ctx_mix_kernel, l),
        grid=(BATCH,),
        in_specs=[
            pl.BlockSpec(memory_space=pltpu.SMEM),
            pl.BlockSpec((SEQ, IN_WIDTH), lambda b: (b, 0)),
        ] + _gmlp_specs(l),
        out_specs=[pl.BlockSpec((SEQ, MIX_WIDTH), lambda b: (b, 0)),
                   kv_spec(H_A), kv_spec(H_A), kv_spec(KV_B), kv_spec(KV_B)],
        out_shape=[jax.ShapeDtypeStruct((N_CTX_TOK, MIX_WIDTH), BF16),
                   kv(H_A), kv(H_A), kv(KV_B), kv(KV_B)],
        compiler_params=_params("parallel"),
        name="ctx_mix",
    )(sink, z, gmlp_w, gmlp_b4, ln_g, ln_b)


def _na_kernel(q_ref, k_ref, v_ref, ck_ref, cv_ref, bias_ref, o_ref):
    ck = ck_ref[...].astype(BF16)
    cv = cv_ref[...].astype(BF16)
    for r in range(GRID_ROWS):
        r0 = min(max(r - NA_ROWS // 2, 0), GRID_ROWS - NA_ROWS)
        band = slice(r0 * GRID_W, (r0 + NA_ROWS) * GRID_W)
        rows = slice(r * GRID_W, (r + 1) * GRID_W)
        q = q_ref[rows, :].astype(BF16)
        s_loc = _dot_nt(q, k_ref[band, :].astype(BF16)) * ATTN_SCALE + bias_ref[r]
        s_ctx = _dot_nt(q, ck) * ATTN_SCALE
        o = _softmax_pv([s_loc, s_ctx], [v_ref[band, :].astype(BF16), cv])
        o_ref[rows, :] = o.astype(BF16)


def _na_bias(rpb):
    r = np.arange(GRID_ROWS)
    row_idx = np.clip(r - NA_ROWS // 2, 0, GRID_ROWS - NA_ROWS)[:, None] + np.arange(NA_ROWS)[None, :]
    col = np.arange(GRID_W)
    c0 = np.clip(col - NA_COLS // 2, 0, GRID_W - NA_COLS)
    col_ok = (col[None, :] >= c0[:, None]) & (col[None, :] < c0[:, None] + NA_COLS)
    idx_r = row_idx - r[:, None] + NA_ROWS - 1
    idx_c = np.clip(col[None, :] - col[:, None] + NA_COLS - 1, 0, 2 * NA_COLS - 2)
    bias = rpb[:, :, idx_r[:, None, :, None], idx_c[None, :, None, :]].astype(F32)
    bias = jnp.where(col_ok[:, None, :], bias, -jnp.inf)
    return bias.reshape(DEPTH, H_A, GRID_ROWS, GRID_W, NA_ROWS * GRID_W)


def _lat_na(z, cache_k, cache_v, bias, l):
    zc = lambda off: pl.BlockSpec((DEC_SEQ, HEAD_DIM), lambda b, h: (b, off // HEAD_DIM + h))
    cache = pl.BlockSpec((None, None, None, PAST_LEN, HEAD_DIM), lambda b, h: (b, l, h, 0, 0))
    return pl.pallas_call(
        _na_kernel,
        grid=(DEC_BATCH, H_A),
        in_specs=[zc(Q_A), zc(K_A), zc(V_A), cache, cache,
                  pl.BlockSpec((None, None, GRID_ROWS, GRID_W, NA_ROWS * GRID_W),
                               lambda b, h: (l, h, 0, 0, 0))],
        out_specs=pl.BlockSpec((DEC_SEQ, HEAD_DIM), lambda b, h: (b, h)),
        out_shape=jax.ShapeDtypeStruct((N_LAT_TOK, W_A), BF16),
        compiler_params=_params("parallel", "parallel"),
        name="lat_na",
    )(z, z, z, cache_k, cache_v, bias)


def _rope(x, cos, sin_signed):
    lane = lax.broadcasted_iota(jnp.int32, x.shape, 1)
    partner = jnp.where(lane % (HEAD_DIM // 2) < HEAD_DIM // 4,
                        pltpu.roll(x, HEAD_DIM - HEAD_DIM // 4, 1),
                        pltpu.roll(x, HEAD_DIM // 4, 1))
    return x * cos + partner * sin_signed


def _rope_tables():
    half = HEAD_DIM // 2
    quarter = half // 2
    t = np.arange(DEC_SEQ)
    freqs = ROPE_BASE ** (-jnp.arange(quarter, dtype=F32) / quarter)

    def tables(pos):
        ang = jnp.asarray(pos, F32)[:, None] * freqs[None, :]
        cos, sin = jnp.cos(ang), jnp.sin(ang)
        return jnp.concatenate([cos, cos], axis=-1), jnp.concatenate([-sin, sin], axis=-1)

    cr, sr = tables(t // GRID_W)
    cc, sc = tables(t % GRID_W)
    return jnp.concatenate([cr, cc], axis=-1), jnp.concatenate([sr, sc], axis=-1)


def _win_kernel(l, sink_ref, q_ref, k_ref, v_ref, ck_ref, cv_ref, cos_ref, sin_ref, o_ref, kr_ref):
    g = pl.program_id(1)
    ck = ck_ref[...].astype(BF16)
    cv = cv_ref[...].astype(BF16)
    kr_ref[...] = _rope(k_ref[...], cos_ref[...], sin_ref[...]).astype(BF16)
    sink = _sink_column(sink_ref, l * H_B + g * REP_B, BLK)
    n_blk = DEC_SEQ // BLK
    for nb in range(n_blk):
        rows = slice(nb * BLK, (nb + 1) * BLK)
        lo = max(nb - 1, 0) * BLK
        hi = min(nb + 2, n_blk) * BLK
        cos = cos_ref[rows, :]
        sin = sin_ref[rows, :]
        q = jnp.concatenate(
            [_rope(q_ref[rows, r * HEAD_DIM:(r + 1) * HEAD_DIM], cos, sin) for r in range(REP_B)],
            axis=0).astype(BF16)
        s_loc = _dot_nt(q, kr_ref[lo:hi, :]) * ATTN_SCALE
        qpos = nb * BLK + lax.broadcasted_iota(jnp.int32, s_loc.shape, 0) % BLK
        kpos = lo + lax.broadcasted_iota(jnp.int32, s_loc.shape, 1)
        s_loc = jnp.where(jnp.abs(qpos - kpos) <= WIN, s_loc, -jnp.inf)
        s_ctx = _dot_nt(q, ck) * ATTN_SCALE
        o = _softmax_pv([s_loc, s_ctx], [v_ref[lo:hi, :].astype(BF16), cv], sink).astype(BF16)
        for r in range(REP_B):
            o_ref[rows, r * HEAD_DIM:(r + 1) * HEAD_DIM] = o[r * BLK:(r + 1) * BLK]


def _lat_win(z, cache_k, cache_v, sink, cos, sin, l):
    zc = lambda off: pl.BlockSpec((DEC_SEQ, HEAD_DIM), lambda b, g: (b, off // HEAD_DIM + g))
    cache = pl.BlockSpec((None, None, None, PAST_LEN, HEAD_DIM), lambda b, g: (b, l, g, 0, 0))
    tab = pl.BlockSpec((DEC_SEQ, HEAD_DIM), lambda b, g: (0, 0))
    qw = REP_B * HEAD_DIM
    return pl.pallas_call(
        functools.partial(_win_kernel, l),
        grid=(DEC_BATCH, KV_B),
        in_specs=[pl.BlockSpec(memory_space=pltpu.SMEM),
                  pl.BlockSpec((DEC_SEQ, qw), lambda b, g: (b, Q_B // qw + g)),
                  zc(K_B), zc(V_B), cache, cache, tab, tab],
        out_specs=pl.BlockSpec((DEC_SEQ, qw), lambda b, g: (b, g)),
        out_shape=jax.ShapeDtypeStruct((N_LAT_TOK, W_B), BF16),
        scratch_shapes=[pltpu.VMEM((DEC_SEQ, HEAD_DIM), BF16)],
        compiler_params=_params("parallel", "parallel"),
        name="lat_win",
    )(sink, z, z, z, cache_k, cache_v, cos, sin)


def _gmlp_kernel(u_ref, v_ref, gw_ref, gb_ref, lng_ref, lnb_ref, o_ref):
    def write(n, g, val):
        o_ref[n * CHUNK:(n + 1) * CHUNK, g * C_GROUP:(g + 1) * C_GROUP] = val.astype(BF16)

    _gmlp(u_ref[...], v_ref[...], gw_ref, gb_ref, lng_ref, lnb_ref, write)


def _lat_gmlp(z, gmlp_w, gmlp_b4, ln_g, ln_b, l):
    rows = DEC_SEQ
    return pl.pallas_call(
        _gmlp_kernel,
        grid=(N_LAT_TOK // rows,),
        in_specs=[pl.BlockSpec((rows, W_C), lambda i: (i, U_C // W_C)),
                  pl.BlockSpec((rows, W_C), lambda i: (i, V_C // W_C))] + _gmlp_specs(l),
        out_specs=pl.BlockSpec((rows, W_C), lambda i: (i, 0)),
        out_shape=jax.ShapeDtypeStruct((N_LAT_TOK, W_C), BF16),
        compiler_params=_params("parallel"),
        name="lat_gmlp",
    )(z, z, gmlp_w, gmlp_b4, ln_g, ln_b)


def _out_kernel(widths, *refs):
    o_refs = refs[:len(widths)]
    w_ref, x_ref, g_ref, gate_ref, xo_ref = refs[len(widths):]
    y = None
    off = 0
    for o_ref, wd in zip(o_refs, widths):
        part = _dot(o_ref[...], w_ref[off:off + wd, :])
        y = part if y is None else y + part
        off += wd
    xo_ref[...] = x_ref[...] + gate_ref[...] * _rms(y, g_ref[...])


def _out_proj(o_list, x, mod5, g_post, w_out, l, row_fn, tm):
    m = x.shape[0]
    widths = tuple(o.shape[1] for o in o_list)
    return pl.pallas_call(
        functools.partial(_out_kernel, widths),
        grid=(m // tm,),
        in_specs=[pl.BlockSpec((tm, wd), lambda i: (i, 0)) for wd in widths] + [
            pl.BlockSpec((None, MIX_WIDTH, D_MODEL), lambda i: (l, 0, 0)),
            pl.BlockSpec((tm, D_MODEL), lambda i: (i, 0)),
            _vec_spec(l),
            _mod_spec(l, 2, row_fn),
        ],
        out_specs=pl.BlockSpec((tm, D_MODEL), lambda i: (i, 0)),
        out_shape=jax.ShapeDtypeStruct((m, D_MODEL), F32),
        compiler_params=_params("parallel"),
        name="out_proj",
    )(*o_list, w_out, x, g_post, mod5)


def _ffn_up_kernel(x_ref, g_ref, sc_ref, sh_ref, wg_ref, wu_ref, a_ref, h_ref):
    @pl.when(pl.program_id(1) == 0)
    def _():
        h = _rms(x_ref[...], g_ref[...]) * (1.0 + sc_ref[...]) + sh_ref[...]
        h_ref[...] = h.astype(BF16)

    h = h_ref[...]
    a_ref[...] = (jax.nn.silu(_dot(h, wg_ref[...])) * _dot(h, wu_ref[...])).astype(BF16)


def _ffn_up(x, mod5, g_pre, w_gate, w_up, l, row_fn, tm):
    m = x.shape[0]
    tn = 512
    wspec = pl.BlockSpec((None, D_MODEL, tn), lambda i, j: (l, 0, j))
    return pl.pallas_call(
        _ffn_up_kernel,
        grid=(m // tm, D_FF // tn),
        in_specs=[
            pl.BlockSpec((tm, D_MODEL), lambda i, j: (i, 0)),
            _vec_spec(l),
            _mod_spec(l, 4, row_fn),
            _mod_spec(l, 3, row_fn),
            wspec, wspec,
        ],
        out_specs=pl.BlockSpec((tm, tn), lambda i, j: (i, j)),
        out_shape=jax.ShapeDtypeStruct((m, D_FF), BF16),
        scratch_shapes=[pltpu.VMEM((tm, D_MODEL), BF16)],
        compiler_params=_params("parallel", "arbitrary"),
        name="ffn_up",
    )(x, g_pre, mod5, mod5, w_gate, w_up)


def _ffn_down_kernel(a_ref, w_ref, x_ref, g_ref, gate_ref, xo_ref, acc_ref):
    k = pl.program_id(1)

    @pl.when(k == 0)
    def _():
        acc_ref[...] = jnp.zeros_like(acc_ref)

    acc_ref[...] += _dot(a_ref[...], w_ref[...])

    @pl.when(k == pl.num_programs(1) - 1)
    def _():
        xo_ref[...] = x_ref[...] + gate_ref[...] * _rms(acc_ref[...], g_ref[...])


def _ffn_down(a, x, mod5, g_post, w_down, l, row_fn, tm):
    m = x.shape[0]
    tk = 1408
    return pl.pallas_call(
        _ffn_down_kernel,
        grid=(m // tm, D_FF // tk),
        in_specs=[
            pl.BlockSpec((tm, tk), lambda i, k: (i, k)),
            pl.BlockSpec((None, tk, D_MODEL), lambda i, k: (l, k, 0)),
            pl.BlockSpec((tm, D_MODEL), lambda i, k: (i, 0)),
            _vec_spec(l),
            _mod_spec(l, 5, row_fn),
        ],
        out_specs=pl.BlockSpec((tm, D_MODEL), lambda i, k: (i, 0)),
        out_shape=jax.ShapeDtypeStruct((m, D_MODEL), F32),
        scratch_shapes=[pltpu.VMEM((tm, D_MODEL), F32)],
        compiler_params=_params("parallel", "arbitrary"),
        name="ffn_down",
    )(a, w_down, x, g_post, mod5)


def kernel(x_prompt, x_sample, cache_a_k, cache_a_v, cache_b_k, cache_b_v, c, c_ctx,
           mod_w, mod_b, norm_mix_pre, norm_mix_post, norm_ffn_pre, norm_ffn_post,
           w_in, w_out, rpb_a, sink_b, gmlp_ln_g, gmlp_ln_b, gmlp_w, gmlp_b,
           w_gate, w_up, w_down):
    cond = jnp.zeros((MOD_ROWS, D_MODEL), F32)
    cond = cond.at[:DEC_BATCH].set(c).at[CTX_MOD_ROW].set(c_ctx)
    mod5 = _modulation(cond, mod_w, mod_b).reshape(DEPTH, MOD_ROWS, N_MOD, 1, D_MODEL)

    vec = lambda a: a.reshape(DEPTH, 1, -1)
    g_mix_pre, g_mix_post = vec(norm_mix_pre), vec(norm_mix_post)
    g_ffn_pre, g_ffn_post = vec(norm_ffn_pre), vec(norm_ffn_post)
    ln_g, ln_b = vec(gmlp_ln_g), vec(gmlp_ln_b)
    gmlp_b4 = gmlp_b.reshape(DEPTH, G_C, CHUNK, 1)
    sink = sink_b.reshape(DEPTH * H_B)
    w_in_b, w_out_b = w_in.astype(BF16), w_out.astype(BF16)
    w_gate_b, w_up_b, w_down_b = w_gate.astype(BF16), w_up.astype(BF16), w_down.astype(BF16)
    bias = _na_bias(rpb_a)
    cos, sin = _rope_tables()

    tm_wide, tm_row = 1024, 512
    ctx_row = lambda i: CTX_MOD_ROW
    lat_row = lambda tm: (lambda i: (i * tm) // DEC_SEQ)

    xp = x_prompt.reshape(N_CTX_TOK, D_MODEL)
    xs = x_sample.reshape(N_LAT_TOK, D_MODEL)
    ka, va, kb, vb = [], [], [], []
    for l in range(DEPTH):
        z = _in_proj(xp, mod5, g_mix_pre, w_in_b, l, ctx_row, tm_wide)
        o, k_a, v_a, k_b, v_b = _ctx_mix(z, sink, gmlp_w, gmlp_b4, ln_g, ln_b, l)
        ka.append(k_a), va.append(v_a), kb.append(k_b), vb.append(v_b)
        xp = _out_proj([o], xp, mod5, g_mix_post, w_out_b, l, ctx_row, tm_row)
        a = _ffn_up(xp, mod5, g_ffn_pre, w_gate_b, w_up_b, l, ctx_row, tm_wide)
        xp = _ffn_down(a, xp, mod5, g_ffn_post, w_down_b, l, ctx_row, tm_row)
        z = _in_proj(xs, mod5, g_mix_pre, w_in_b, l, lat_row(tm_wide), tm_wide)
        o_a = _lat_na(z, cache_a_k, cache_a_v, bias, l)
        o_b = _lat_win(z, cache_b_k, cache_b_v, sink, cos, sin, l)
        o_c = _lat_gmlp(z, gmlp_w, gmlp_b4, ln_g, ln_b, l)
        xs = _out_proj([o_a, o_b, o_c], xs, mod5, g_mix_post, w_out_b, l, lat_row(tm_row), tm_row)
        a = _ffn_up(xs, mod5, g_ffn_pre, w_gate_b, w_up_b, l, lat_row(tm_wide), tm_wide)
        xs = _ffn_down(a, xs, mod5, g_ffn_post, w_down_b, l, lat_row(tm_row), tm_row)

    return (xp.reshape(BATCH, SEQ, D_MODEL), xs.reshape(DEC_BATCH, DEC_SEQ, D_MODEL),
            jnp.stack(ka, axis=1), jnp.stack(va, axis=1),
            jnp.stack(kb, axis=1), jnp.stack(vb, axis=1))
```

```python
import functools

import jax
import jax.numpy as jnp
import numpy as np
from jax import lax
from jax.experimental import pallas as pl
from jax.experimental.pallas import tpu as pltpu

D_MODEL = 2048
BATCH = 32
SEQ = 256
DEPTH = 4
DEC_BATCH = 8
DEC_SEQ = 1024
PAST_LEN = 256
GRID_W = 64
GRID_ROWS = DEC_SEQ // GRID_W
HEAD_DIM = 128
H_A = 6
H_B = 6
KV_B = 2
REP_B = H_B // KV_B
G_C = 4
C_GROUP = 128
W_A = H_A * HEAD_DIM
W_B = H_B * HEAD_DIM
W_C = G_C * C_GROUP
MIX_WIDTH = W_A + W_B + W_C
NA_ROWS = 8
NA_COLS = 16
WIN = 128
BLK = 128
CHUNK = 128
D_FF = 5632
N_MOD = 6
ROPE_BASE = 10000.0
NORM_EPS = 1e-6
ATTN_SCALE = HEAD_DIM ** -0.5
Q_A = 0
K_A = Q_A + W_A
V_A = K_A + W_A
Q_B = V_A + W_A
K_B = Q_B + W_B
V_B = K_B + KV_B * HEAD_DIM
U_C = V_B + KV_B * HEAD_DIM
V_C = U_C + W_C
IN_WIDTH = V_C + W_C

N_CTX_TOK = BATCH * SEQ
N_LAT_TOK = DEC_BATCH * DEC_SEQ
MOD_ROWS = 16
CTX_MOD_ROW = DEC_BATCH

VMEM_LIMIT_BYTES = 56 * 1024 * 1024

BF16 = jnp.bfloat16
F32 = jnp.float32


def _params(*sem):
    return pltpu.CompilerParams(dimension_semantics=sem, vmem_limit_bytes=VMEM_LIMIT_BYTES)


def _dot(a, b):
    return jnp.dot(a, b, preferred_element_type=F32)


def _dot_nt(a, b):
    return lax.dot_general(a, b, (((1,), (1,)), ((), ())), preferred_element_type=F32)


def _rms(x, g):
    return (x * lax.rsqrt(jnp.mean(x * x, axis=-1, keepdims=True) + NORM_EPS)) * g


def _mod_kernel(c_ref, w_ref, b_ref, o_ref):
    s = jax.nn.silu(c_ref[...])
    o_ref[...] = _dot(s.astype(BF16), w_ref[...].astype(BF16)) + b_ref[...]


def _modulation(cond, mod_w, mod_b):
    tn = 1024
    n = N_MOD * D_MODEL
    return pl.pallas_call(
        _mod_kernel,
        grid=(DEPTH, n // tn),
        in_specs=[
            pl.BlockSpec((MOD_ROWS, D_MODEL), lambda l, j: (0, 0)),
            pl.BlockSpec((None, D_MODEL, tn), lambda l, j: (l, 0, j)),
            pl.BlockSpec((None, 1, tn), lambda l, j: (l, 0, j)),
        ],
        out_specs=pl.BlockSpec((None, MOD_ROWS, tn), lambda l, j: (l, 0, j)),
        out_shape=jax.ShapeDtypeStruct((DEPTH, MOD_ROWS, n), F32),
        compiler_params=_params("parallel", "parallel"),
        name="modulation",
    )(cond, mod_w, mod_b.reshape(DEPTH, 1, n))


def _mod_spec(l, which, row_fn):
    return pl.BlockSpec((None, None, None, 1, D_MODEL),
                        lambda i, *_: (l, row_fn(i), which, 0, 0))


def _vec_spec(l):
    return pl.BlockSpec((None, 1, D_MODEL), lambda i, *_: (l, 0, 0))


def _modulated_norm(x, g, scale, shift):
    return (_rms(x, g) * (1.0 + scale) + shift).astype(BF16)


def _norm_kernel(x_ref, g_ref, sc_ref, sh_ref, h_ref):
    h_ref[...] = _modulated_norm(x_ref[...], g_ref[...], sc_ref[...], sh_ref[...])


def _norm_mod(x, mod5, g_pre, l, row_fn, tm):
    m = x.shape[0]
    return pl.pallas_call(
        _norm_kernel,
        grid=(m // tm,),
        in_specs=[pl.BlockSpec((tm, D_MODEL), lambda i: (i, 0)),
                  _vec_spec(l), _mod_spec(l, 1, row_fn), _mod_spec(l, 0, row_fn)],
        out_specs=pl.BlockSpec((tm, D_MODEL), lambda i: (i, 0)),
        out_shape=jax.ShapeDtypeStruct((m, D_MODEL), BF16),
        compiler_params=_params("parallel"),
        name="norm_mod",
    )(x, g_pre, mod5, mod5)


def _in_kernel(h_ref, w_ref, z_ref):
    z_ref[...] = _dot(h_ref[...], w_ref[...])


def _in_proj(h, w_in, l, tm):
    m = h.shape[0]
    tn = 1152
    return pl.pallas_call(
        _in_kernel,
        grid=(IN_WIDTH // tn, m // tm),
        in_specs=[
            pl.BlockSpec((tm, D_MODEL), lambda j, i: (i, 0)),
            pl.BlockSpec((None, D_MODEL, tn), lambda j, i: (l, 0, j)),
        ],
        out_specs=pl.BlockSpec((tm, tn), lambda j, i: (i, j)),
        out_shape=jax.ShapeDtypeStruct((m, IN_WIDTH), F32),
        compiler_params=_params("parallel", "parallel"),
        name="in_proj",
    )(h, w_in)


def _softmax_pv(scores, values, extra_logit=None):
    m = scores[0].max(axis=-1, keepdims=True)
    for s in scores[1:]:
        m = jnp.maximum(m, s.max(axis=-1, keepdims=True))
    if extra_logit is not None:
        m = jnp.maximum(m, extra_logit)
    den = None
    acc = None
    for s, v in zip(scores, values):
        e = jnp.exp(s - m)
        d = e.sum(axis=-1, keepdims=True)
        a = _dot(e.astype(BF16), v)
        den = d if den is None else den + d
        acc = a if acc is None else acc + a
    if extra_logit is not None:
        den = den + jnp.exp(extra_logit - m)
    return acc / den


def _sink_column(sink_ref, base, rows_per_head):
    rows = lax.broadcasted_iota(jnp.int32, (REP_B * rows_per_head, 1), 0)
    col = jnp.full((REP_B * rows_per_head, 1), sink_ref[base], F32)
    for r in range(1, REP_B):
        col = jnp.where(rows >= r * rows_per_head, sink_ref[base + r], col)
    return col


def _gmlp(u_raw, v_raw, gw_ref, gb_ref, lng_ref, lnb_ref, write):
    u = jax.nn.gelu(u_raw)
    v = jax.nn.gelu(v_raw)
    mu = jnp.mean(v, axis=-1, keepdims=True)
    xc = v - mu
    vn = xc * lax.rsqrt(jnp.mean(xc * xc, axis=-1, keepdims=True) + NORM_EPS)
    vn = (vn * lng_ref[...] + lnb_ref[...]).astype(BF16)
    for n in range(u_raw.shape[0] // CHUNK):
        rs = slice(n * CHUNK, (n + 1) * CHUNK)
        for g in range(G_C):
            cs = slice(g * C_GROUP, (g + 1) * C_GROUP)
            s = _dot(gw_ref[g].astype(BF16), vn[rs, cs]) + gb_ref[g]
            write(n, g, u[rs, cs] * s)


def _ctx_mix_kernel(l, sink_ref, z_ref, gw_ref, gb_ref, lng_ref, lnb_ref,
                    o_ref, ka_ref, va_ref, kb_ref, vb_ref):
    def col(off, i=0):
        return z_ref[:, off + i * HEAD_DIM: off + (i + 1) * HEAD_DIM]

    for h in range(H_A):
        k = col(K_A, h)
        v = col(V_A, h)
        ka_ref[h] = k
        va_ref[h] = v
        s = _dot_nt(col(Q_A, h).astype(BF16), k.astype(BF16)) * ATTN_SCALE
        o_ref[:, h * HEAD_DIM:(h + 1) * HEAD_DIM] = _softmax_pv([s], [v.astype(BF16)]).astype(BF16)

    for g in range(KV_B):
        k = col(K_B, g)
        v = col(V_B, g)
        kb_ref[g] = k
        vb_ref[g] = v
        q = jnp.concatenate([col(Q_B, g * REP_B + r) for r in range(REP_B)], axis=0).astype(BF16)
        s = _dot_nt(q, k.astype(BF16)) * ATTN_SCALE
        sink = _sink_column(sink_ref, l * H_B + g * REP_B, SEQ)
        o = _softmax_pv([s], [v.astype(BF16)], sink).astype(BF16)
        for r in range(REP_B):
            c0 = W_A + (g * REP_B + r) * HEAD_DIM
            o_ref[:, c0:c0 + HEAD_DIM] = o[r * SEQ:(r + 1) * SEQ]

    def write(n, g, val):
        c0 = W_A + W_B + g * C_GROUP
        o_ref[n * CHUNK:(n + 1) * CHUNK, c0:c0 + C_GROUP] = val.astype(BF16)

    _gmlp(z_ref[:, U_C:V_C], z_ref[:, V_C:IN_WIDTH], gw_ref, gb_ref, lng_ref, lnb_ref, write)


def _gmlp_specs(l):
    return [
        pl.BlockSpec((None, G_C, CHUNK, CHUNK), lambda i, *_: (l, 0, 0, 0)),
        pl.BlockSpec((None, G_C, CHUNK, 1), lambda i, *_: (l, 0, 0, 0)),
        pl.BlockSpec((None, 1, W_C), lambda i, *_: (l, 0, 0)),
        pl.BlockSpec((None, 1, W_C), lambda i, *_: (l, 0, 0)),
    ]


def _ctx_mix(z, sink, gmlp_w, gmlp_b4, ln_g, ln_b, l):
    kv = lambda nh: jax.ShapeDtypeStruct((BATCH, nh, SEQ, HEAD_DIM), F32)
    kv_spec = lambda nh: pl.BlockSpec((None, nh, SEQ, HEAD_DIM), lambda b: (b, 0, 0, 0))
    return pl.pallas_call(
        functools.partial(_ctx_mix_kernel, l),
        grid=(BATCH,),
        in_specs=[
            pl.BlockSpec(memory_space=pltpu.SMEM),
            pl.BlockSpec((SEQ, IN_WIDTH), lambda b: (b, 0)),
        ] + _gmlp_specs(l),
        out_specs=[pl.BlockSpec((SEQ, MIX_WIDTH), lambda b: (b, 0)),
                   kv_spec(H_A), kv_spec(H_A), kv_spec(KV_B), kv_spec(KV_B)],
        out_shape=[jax.ShapeDtypeStruct((N_CTX_TOK, MIX_WIDTH), BF16),
                   kv(H_A), kv(H_A), kv(KV_B), kv(KV_B)],
        compiler_params=_params("parallel"),
        name="ctx_mix",
    )(sink, z, gmlp_w, gmlp_b4, ln_g, ln_b)


NA_QROWS = 4
NA_WROWS = 12
NA_QBLOCKS = GRID_ROWS // NA_QROWS


def _na_band_start(r):
    return min(max(r - NA_ROWS // 2, 0), GRID_ROWS - NA_ROWS)


def _na_window_start(qb):
    return min(max(qb * NA_QROWS - NA_ROWS // 2, 0), GRID_ROWS - NA_WROWS)


def _na_kernel(q_ref, k_ref, v_ref, ck_ref, cv_ref, bias_ref, o_ref):
    ck = ck_ref[...].astype(BF16)
    cv = cv_ref[...].astype(BF16)
    for qb in range(NA_QBLOCKS):
        w0 = _na_window_start(qb)
        win = slice(w0 * GRID_W, (w0 + NA_WROWS) * GRID_W)
        rows = slice(qb * NA_QROWS * GRID_W, (qb + 1) * NA_QROWS * GRID_W)
        q = q_ref[rows, :].astype(BF16)
        s_loc = _dot_nt(q, k_ref[win, :].astype(BF16)) * ATTN_SCALE + bias_ref[qb]
        s_ctx = _dot_nt(q, ck) * ATTN_SCALE
        o = _softmax_pv([s_loc, s_ctx], [v_ref[win, :].astype(BF16), cv])
        o_ref[rows, :] = o.astype(BF16)


def _na_bias(rpb):
    n_dc = 2 * NA_COLS - 1
    edge = GRID_W - NA_COLS
    ext = jnp.concatenate([jnp.repeat(rpb[..., :1], edge, axis=-1), rpb,
                           jnp.repeat(rpb[..., -1:], edge, axis=-1)], axis=-1).astype(F32)
    toep = jnp.stack([ext[..., GRID_W - 1 - q: 2 * GRID_W - 1 - q] for q in range(GRID_W)], axis=-2)
    col = np.arange(GRID_W)
    c0 = np.clip(col - NA_COLS // 2, 0, GRID_W - NA_COLS)
    col_ok = (col[None, :] >= c0[:, None]) & (col[None, :] < c0[:, None] + NA_COLS)
    toep = jnp.where(col_ok, toep, -jnp.inf)
    masked = jnp.full(toep.shape[:2] + (GRID_W, GRID_W), -jnp.inf, F32)
    blocks = []
    for qb in range(NA_QBLOCKS):
        w0 = _na_window_start(qb)
        q_rows = []
        for r in range(qb * NA_QROWS, (qb + 1) * NA_QROWS):
            r0 = _na_band_start(r)
            tiles = [toep[:, :, kr - r + NA_ROWS - 1] if r0 <= kr < r0 + NA_ROWS else masked
                     for kr in range(w0, w0 + NA_WROWS)]
            q_rows.append(jnp.concatenate(tiles, axis=-1))
        blocks.append(jnp.concatenate(q_rows, axis=-2))
    return jnp.stack(blocks, axis=2)


def _lat_na(z, cache_k, cache_v, bias, l):
    zc = lambda off: pl.BlockSpec((DEC_SEQ, HEAD_DIM), lambda h, b: (b, off // HEAD_DIM + h))
    cache = pl.BlockSpec((None, None, None, PAST_LEN, HEAD_DIM), lambda h, b: (b, l, h, 0, 0))
    return pl.pallas_call(
        _na_kernel,
        grid=(H_A, DEC_BATCH),
        in_specs=[zc(Q_A), zc(K_A), zc(V_A), cache, cache,
                  pl.BlockSpec((None, None, NA_QBLOCKS, NA_QROWS * GRID_W, NA_WROWS * GRID_W),
                               lambda h, b: (l, h, 0, 0, 0))],
        out_specs=pl.BlockSpec((DEC_SEQ, HEAD_DIM), lambda h, b: (b, h)),
        out_shape=jax.ShapeDtypeStruct((N_LAT_TOK, W_A), BF16),
        compiler_params=_params("parallel", "parallel"),
        name="lat_na",
    )(z, z, z, cache_k, cache_v, bias)


def _rope(x, cos, sin_signed):
    lane = lax.broadcasted_iota(jnp.int32, x.shape, 1)
    partner = jnp.where(lane % (HEAD_DIM // 2) < HEAD_DIM // 4,
                        pltpu.roll(x, HEAD_DIM - HEAD_DIM // 4, 1),
                        pltpu.roll(x, HEAD_DIM // 4, 1))
    return x * cos + partner * sin_signed


def _rope_tables():
    half = HEAD_DIM // 2
    quarter = half // 2
    t = np.arange(DEC_SEQ)
    freqs = ROPE_BASE ** (-jnp.arange(quarter, dtype=F32) / quarter)

    def tables(pos):
        ang = jnp.asarray(pos, F32)[:, None] * freqs[None, :]
        cos, sin = jnp.cos(ang), jnp.sin(ang)
        return jnp.concatenate([cos, cos], axis=-1), jnp.concatenate([-sin, sin], axis=-1)

    cr, sr = tables(t // GRID_W)
    cc, sc = tables(t % GRID_W)
    return jnp.concatenate([cr, cc], axis=-1), jnp.concatenate([sr, sc], axis=-1)


def _win_kernel(l, sink_ref, q_ref, k_ref, v_ref, ck_ref, cv_ref, cos_ref, sin_ref, o_ref, kr_ref):
    g = pl.program_id(1)
    ck = ck_ref[...].astype(BF16)
    cv = cv_ref[...].astype(BF16)
    kr_ref[...] = _rope(k_ref[...], cos_ref[...], sin_ref[...]).astype(BF16)
    sink = _sink_column(sink_ref, l * H_B + g * REP_B, BLK)
    n_blk = DEC_SEQ // BLK
    for nb in range(n_blk):
        rows = slice(nb * BLK, (nb + 1) * BLK)
        lo = max(nb - 1, 0) * BLK
        hi = min(nb + 2, n_blk) * BLK
        cos = cos_ref[rows, :]
        sin = sin_ref[rows, :]
        q = jnp.concatenate(
            [_rope(q_ref[rows, r * HEAD_DIM:(r + 1) * HEAD_DIM], cos, sin) for r in range(REP_B)],
            axis=0).astype(BF16)
        s_loc = _dot_nt(q, kr_ref[lo:hi, :]) * ATTN_SCALE
        qpos = nb * BLK + lax.broadcasted_iota(jnp.int32, s_loc.shape, 0) % BLK
        kpos = lo + lax.broadcasted_iota(jnp.int32, s_loc.shape, 1)
        s_loc = jnp.where(jnp.abs(qpos - kpos) <= WIN, s_loc, -jnp.inf)
        s_ctx = _dot_nt(q, ck) * ATTN_SCALE
        o = _softmax_pv([s_loc, s_ctx], [v_ref[lo:hi, :].astype(BF16), cv], sink).astype(BF16)
        for r in range(REP_B):
            o_ref[rows, r * HEAD_DIM:(r + 1) * HEAD_DIM] = o[r * BLK:(r + 1) * BLK]


def _lat_win(z, cache_k, cache_v, sink, cos, sin, l):
    zc = lambda off: pl.BlockSpec((DEC_SEQ, HEAD_DIM), lambda b, g: (b, off // HEAD_DIM + g))
    cache = pl.BlockSpec((None, None, None, PAST_LEN, HEAD_DIM), lambda b, g: (b, l, g, 0, 0))
    tab = pl.BlockSpec((DEC_SEQ, HEAD_DIM), lambda b, g: (0, 0))
    qw = REP_B * HEAD_DIM
    return pl.pallas_call(
        functools.partial(_win_kernel, l),
        grid=(DEC_BATCH, KV_B),
        in_specs=[pl.BlockSpec(memory_space=pltpu.SMEM),
                  pl.BlockSpec((DEC_SEQ, qw), lambda b, g: (b, Q_B // qw + g)),
                  zc(K_B), zc(V_B), cache, cache, tab, tab],
        out_specs=pl.BlockSpec((DEC_SEQ, qw), lambda b, g: (b, g)),
        out_shape=jax.ShapeDtypeStruct((N_LAT_TOK, W_B), BF16),
        scratch_shapes=[pltpu.VMEM((DEC_SEQ, HEAD_DIM), BF16)],
        compiler_params=_params("parallel", "parallel"),
        name="lat_win",
    )(sink, z, z, z, cache_k, cache_v, cos, sin)


def _gmlp_kernel(u_ref, v_ref, gw_ref, gb_ref, lng_ref, lnb_ref, o_ref):
    def write(n, g, val):
        o_ref[n * CHUNK:(n + 1) * CHUNK, g * C_GROUP:(g + 1) * C_GROUP] = val.astype(BF16)

    _gmlp(u_ref[...], v_ref[...], gw_ref, gb_ref, lng_ref, lnb_ref, write)


def _lat_gmlp(z, gmlp_w, gmlp_b4, ln_g, ln_b, l):
    rows = DEC_SEQ
    return pl.pallas_call(
        _gmlp_kernel,
        grid=(N_LAT_TOK // rows,),
        in_specs=[pl.BlockSpec((rows, W_C), lambda i: (i, U_C // W_C)),
                  pl.BlockSpec((rows, W_C), lambda i: (i, V_C // W_C))] + _gmlp_specs(l),
        out_specs=pl.BlockSpec((rows, W_C), lambda i: (i, 0)),
        out_shape=jax.ShapeDtypeStruct((N_LAT_TOK, W_C), BF16),
        compiler_params=_params("parallel"),
        name="lat_gmlp",
    )(z, z, gmlp_w, gmlp_b4, ln_g, ln_b)


EPI_ROWS = 256


def _residual_epilogue(y, rows, x_ref, g_post_ref, gate_ref, xo_ref, next_norm):
    x_new = x_ref[rows, :] + gate_ref[...] * _rms(y, g_post_ref[...])
    xo_ref[rows, :] = x_new
    if next_norm is not None:
        g_ref, sc_ref, sh_ref, h_ref = next_norm
        h_ref[rows, :] = _modulated_norm(x_new, g_ref[...], sc_ref[...], sh_ref[...])


def _out_kernel(widths, *refs):
    o_refs = refs[:len(widths)]
    (w_ref, x_ref, g_post_ref, gate_ref, g_next_ref, sc_ref, sh_ref,
     xo_ref, h_ref) = refs[len(widths):]
    for r in range(x_ref.shape[0] // EPI_ROWS):
        rows = slice(r * EPI_ROWS, (r + 1) * EPI_ROWS)
        y = None
        off = 0
        for o_ref, wd in zip(o_refs, widths):
            part = _dot(o_ref[rows, :], w_ref[off:off + wd, :])
            y = part if y is None else y + part
            off += wd
        _residual_epilogue(y, rows, x_ref, g_post_ref, gate_ref, xo_ref,
                           (g_next_ref, sc_ref, sh_ref, h_ref))


def _out_proj(o_list, x, mod5, g_post, g_ffn_pre, w_out, l, row_fn, tm):
    m = x.shape[0]
    widths = tuple(o.shape[1] for o in o_list)
    row_spec = pl.BlockSpec((tm, D_MODEL), lambda i: (i, 0))
    return pl.pallas_call(
        functools.partial(_out_kernel, widths),
        grid=(m // tm,),
        in_specs=[pl.BlockSpec((tm, wd), lambda i: (i, 0)) for wd in widths] + [
            pl.BlockSpec((None, MIX_WIDTH, D_MODEL), lambda i: (l, 0, 0)),
            row_spec,
            _vec_spec(l),
            _mod_spec(l, 2, row_fn),
            _vec_spec(l),
            _mod_spec(l, 4, row_fn),
            _mod_spec(l, 3, row_fn),
        ],
        out_specs=[row_spec, row_spec],
        out_shape=[jax.ShapeDtypeStruct((m, D_MODEL), F32),
                   jax.ShapeDtypeStruct((m, D_MODEL), BF16)],
        compiler_params=_params("parallel"),
        name="out_proj",
    )(*o_list, w_out, x, g_post, mod5, g_ffn_pre, mod5, mod5)


FFN_UP_TN = 1408
FFN_UP_COLS = 512
FFN_DOWN_TK = 1408


def _ffn_up_kernel(h_ref, wg_ref, wu_ref, a_ref):
    h = h_ref[...]
    for c0 in range(0, FFN_UP_TN, FFN_UP_COLS):
        cols = slice(c0, min(c0 + FFN_UP_COLS, FFN_UP_TN))
        a_ref[:, cols] = (jax.nn.silu(_dot(h, wg_ref[:, cols])) * _dot(h, wu_ref[:, cols])).astype(BF16)


def _ffn_up(h, w_gate, w_up, l, tm):
    m = h.shape[0]
    tn = FFN_UP_TN
    wspec = pl.BlockSpec((None, D_MODEL, tn), lambda j, i: (l, 0, j))
    return pl.pallas_call(
        _ffn_up_kernel,
        grid=(D_FF // tn, m // tm),
        in_specs=[pl.BlockSpec((tm, D_MODEL), lambda j, i: (i, 0)), wspec, wspec],
        out_specs=pl.BlockSpec((tm, tn), lambda j, i: (i, j)),
        out_shape=jax.ShapeDtypeStruct((m, D_FF), BF16),
        compiler_params=_params("parallel", "parallel"),
        name="ffn_up",
    )(h, w_gate, w_up)


def _ffn_down_kernel(with_next, a_ref, w_ref, x_ref, g_post_ref, gate_ref, *refs):
    if with_next:
        g_next_ref, sc_ref, sh_ref, xo_ref, h_ref, acc_ref = refs
        next_norm = (g_next_ref, sc_ref, sh_ref, h_ref)
    else:
        xo_ref, acc_ref = refs
        next_norm = None
    k = pl.program_id(1)
    last = pl.num_programs(1) - 1

    @pl.when(k == 0)
    def _():
        acc_ref[...] = _dot(a_ref[...], w_ref[...])

    @pl.when((k > 0) & (k < last))
    def _():
        acc_ref[...] += _dot(a_ref[...], w_ref[...])

    @pl.when(k == last)
    def _():
        for r in range(x_ref.shape[0] // EPI_ROWS):
            rows = slice(r * EPI_ROWS, (r + 1) * EPI_ROWS)
            y = acc_ref[rows, :] + _dot(a_ref[rows, :], w_ref[...])
            _residual_epilogue(y, rows, x_ref, g_post_ref, gate_ref, xo_ref, next_norm)


def _ffn_down(a, x, mod5, g_post, g_mix_pre, w_down, l, row_fn, tm):
    m = x.shape[0]
    tk = FFN_DOWN_TK
    with_next = l + 1 < DEPTH
    row_spec = pl.BlockSpec((tm, D_MODEL), lambda i, k: (i, 0))
    in_specs = [
        pl.BlockSpec((tm, tk), lambda i, k: (i, k)),
        pl.BlockSpec((None, tk, D_MODEL), lambda i, k: (l, k, 0)),
        row_spec,
        _vec_spec(l),
        _mod_spec(l, 5, row_fn),
    ]
    args = [a, w_down, x, g_post, mod5]
    out_specs = [row_spec]
    out_shape = [jax.ShapeDtypeStruct((m, D_MODEL), F32)]
    if with_next:
        in_specs += [_vec_spec(l + 1), _mod_spec(l + 1, 1, row_fn), _mod_spec(l + 1, 0, row_fn)]
        args += [g_mix_pre, mod5, mod5]
        out_specs.append(row_spec)
        out_shape.append(jax.ShapeDtypeStruct((m, D_MODEL), BF16))
    outs = pl.pallas_call(
        functools.partial(_ffn_down_kernel, with_next),
        grid=(m // tm, D_FF // tk),
        in_specs=in_specs,
        out_specs=out_specs,
        out_shape=out_shape,
        scratch_shapes=[pltpu.VMEM((tm, D_MODEL), F32)],
        compiler_params=_params("parallel", "arbitrary"),
        name="ffn_down",
    )(*args)
    return outs if with_next else (outs[0], None)


def kernel(x_prompt, x_sample, cache_a_k, cache_a_v, cache_b_k, cache_b_v, c, c_ctx,
           mod_w, mod_b, norm_mix_pre, norm_mix_post, norm_ffn_pre, norm_ffn_post,
           w_in, w_out, rpb_a, sink_b, gmlp_ln_g, gmlp_ln_b, gmlp_w, gmlp_b,
           w_gate, w_up, w_down):
    cond = jnp.zeros((MOD_ROWS, D_MODEL), F32)
    cond = cond.at[:DEC_BATCH].set(c).at[CTX_MOD_ROW].set(c_ctx)
    mod5 = _modulation(cond, mod_w, mod_b).reshape(DEPTH, MOD_ROWS, N_MOD, 1, D_MODEL)

    vec = lambda a: a.reshape(DEPTH, 1, -1)
    g_mix_pre, g_mix_post = vec(norm_mix_pre), vec(norm_mix_post)
    g_ffn_pre, g_ffn_post = vec(norm_ffn_pre), vec(norm_ffn_post)
    ln_g, ln_b = vec(gmlp_ln_g), vec(gmlp_ln_b)
    gmlp_b4 = gmlp_b.reshape(DEPTH, G_C, CHUNK, 1)
    sink = sink_b.reshape(DEPTH * H_B)
    w_in_b, w_out_b = w_in.astype(BF16), w_out.astype(BF16)
    w_gate_b, w_up_b, w_down_b = w_gate.astype(BF16), w_up.astype(BF16), w_down.astype(BF16)
    bias = _na_bias(rpb_a)
    cos, sin = _rope_tables()

    tm_wide, tm_row = 1024, 512
    ctx_row = lambda i: CTX_MOD_ROW
    lat_row = lambda tm: (lambda i: (i * tm) // DEC_SEQ)

    xp = x_prompt.reshape(N_CTX_TOK, D_MODEL)
    xs = x_sample.reshape(N_LAT_TOK, D_MODEL)
    hp = _norm_mod(xp, mod5, g_mix_pre, 0, ctx_row, tm_row)
    hs = _norm_mod(xs, mod5, g_mix_pre, 0, lat_row(tm_row), tm_row)
    ka, va, kb, vb = [], [], [], []
    for l in range(DEPTH):
        z = _in_proj(hp, w_in_b, l, tm_wide)
        o, k_a, v_a, k_b, v_b = _ctx_mix(z, sink, gmlp_w, gmlp_b4, ln_g, ln_b, l)
        ka.append(k_a), va.append(v_a), kb.append(k_b), vb.append(v_b)
        xp, hp = _out_proj([o], xp, mod5, g_mix_post, g_ffn_pre, w_out_b, l, ctx_row, tm_row)
        a = _ffn_up(hp, w_gate_b, w_up_b, l, tm_wide)
        xp, hp = _ffn_down(a, xp, mod5, g_ffn_post, g_mix_pre, w_down_b, l, ctx_row, tm_row)
        z = _in_proj(hs, w_in_b, l, tm_wide)
        o_a = _lat_na(z, cache_a_k, cache_a_v, bias, l)
        o_b = _lat_win(z, cache_b_k, cache_b_v, sink, cos, sin, l)
        o_c = _lat_gmlp(z, gmlp_w, gmlp_b4, ln_g, ln_b, l)
        xs, hs = _out_proj([o_a, o_b, o_c], xs, mod5, g_mix_post, g_ffn_pre, w_out_b, l,
                           lat_row(tm_row), tm_row)
        a = _ffn_up(hs, w_gate_b, w_up_b, l, tm_wide)
        xs, hs = _ffn_down(a, xs, mod5, g_ffn_post, g_mix_pre, w_down_b, l, lat_row(tm_row), tm_row)

    return (xp.reshape(BATCH, SEQ, D_MODEL), xs.reshape(DEC_BATCH, DEC_SEQ, D_MODEL),
            jnp.stack(ka, axis=1), jnp.stack(va, axis=1),
            jnp.stack(kb, axis=1), jnp.stack(vb, axis=1))
```

```python
import functools

import jax
import jax.numpy as jnp
import numpy as np
from jax import lax
from jax.experimental import pallas as pl
from jax.experimental.pallas import tpu as pltpu

D_MODEL = 2048
BATCH = 32
SEQ = 256
DEPTH = 4
DEC_BATCH = 8
DEC_SEQ = 1024
PAST_LEN = 256
GRID_W = 64
GRID_ROWS = DEC_SEQ // GRID_W
HEAD_DIM = 128
H_A = 6
H_B = 6
KV_B = 2
REP_B = H_B // KV_B
G_C = 4
C_GROUP = 128
W_A = H_A * HEAD_DIM
W_B = H_B * HEAD_DIM
W_C = G_C * C_GROUP
MIX_WIDTH = W_A + W_B + W_C
NA_ROWS = 8
NA_COLS = 16
WIN = 128
BLK = 128
CHUNK = 128
D_FF = 5632
N_MOD = 6
ROPE_BASE = 10000.0
NORM_EPS = 1e-6
ATTN_SCALE = HEAD_DIM ** -0.5
Q_A = 0
K_A = Q_A + W_A
V_A = K_A + W_A
Q_B = V_A + W_A
K_B = Q_B + W_B
V_B = K_B + KV_B * HEAD_DIM
U_C = V_B + KV_B * HEAD_DIM
V_C = U_C + W_C
IN_WIDTH = V_C + W_C

N_CTX_TOK = BATCH * SEQ
N_LAT_TOK = DEC_BATCH * DEC_SEQ
MOD_ROWS = 16
CTX_MOD_ROW = DEC_BATCH

VMEM_LIMIT_BYTES = 58 * 1024 * 1024
MXU_TILE = 256

BF16 = jnp.bfloat16
F32 = jnp.float32


def _params(*sem):
    return pltpu.CompilerParams(dimension_semantics=sem, vmem_limit_bytes=VMEM_LIMIT_BYTES)


def _dot(a, b):
    return jnp.dot(a, b, preferred_element_type=F32)


def _dot_nt(a, b):
    return lax.dot_general(a, b, (((1,), (1,)), ((), ())), preferred_element_type=F32)


def _rms(x, g):
    return (x * lax.rsqrt(jnp.mean(x * x, axis=-1, keepdims=True) + NORM_EPS)) * g


def _mod_kernel(c_ref, w_ref, b_ref, o_ref):
    s = jax.nn.silu(c_ref[...])
    o_ref[...] = _dot(s.astype(BF16), w_ref[...].astype(BF16)) + b_ref[...]


def _modulation(cond, mod_w, mod_b):
    tn = 1024
    n = N_MOD * D_MODEL
    return pl.pallas_call(
        _mod_kernel,
        grid=(DEPTH, n // tn),
        in_specs=[
            pl.BlockSpec((MOD_ROWS, D_MODEL), lambda l, j: (0, 0)),
            pl.BlockSpec((None, D_MODEL, tn), lambda l, j: (l, 0, j)),
            pl.BlockSpec((None, 1, tn), lambda l, j: (l, 0, j)),
        ],
        out_specs=pl.BlockSpec((None, MOD_ROWS, tn), lambda l, j: (l, 0, j)),
        out_shape=jax.ShapeDtypeStruct((DEPTH, MOD_ROWS, n), F32),
        compiler_params=_params("parallel", "parallel"),
        name="modulation",
    )(cond, mod_w, mod_b.reshape(DEPTH, 1, n))


def _mod_spec(l, which, row_fn):
    return pl.BlockSpec((None, None, None, 1, D_MODEL),
                        lambda i, *_: (l, row_fn(i), which, 0, 0))


def _vec_spec(l):
    return pl.BlockSpec((None, 1, D_MODEL), lambda i, *_: (l, 0, 0))


def _modulated_norm(x, g, scale, shift):
    return (_rms(x, g) * (1.0 + scale) + shift).astype(BF16)


def _norm_kernel(x_ref, g_ref, sc_ref, sh_ref, h_ref):
    h_ref[...] = _modulated_norm(x_ref[...], g_ref[...], sc_ref[...], sh_ref[...])


def _norm_mod(x, mod5, g_pre, l, row_fn, tm):
    m = x.shape[0]
    return pl.pallas_call(
        _norm_kernel,
        grid=(m // tm,),
        in_specs=[pl.BlockSpec((tm, D_MODEL), lambda i: (i, 0)),
                  _vec_spec(l), _mod_spec(l, 1, row_fn), _mod_spec(l, 0, row_fn)],
        out_specs=pl.BlockSpec((tm, D_MODEL), lambda i: (i, 0)),
        out_shape=jax.ShapeDtypeStruct((m, D_MODEL), BF16),
        compiler_params=_params("parallel"),
        name="norm_mod",
    )(x, g_pre, mod5, mod5)


def _in_kernel(h_ref, w_ref, z_ref):
    z_ref[...] = _dot(h_ref[...], w_ref[...])


def _in_proj(h, w_in, l, tm):
    m = h.shape[0]
    tn = 6 * MXU_TILE
    return pl.pallas_call(
        _in_kernel,
        grid=(IN_WIDTH // tn, m // tm),
        in_specs=[
            pl.BlockSpec((tm, D_MODEL), lambda j, i: (i, 0)),
            pl.BlockSpec((None, D_MODEL, tn), lambda j, i: (l, 0, j)),
        ],
        out_specs=pl.BlockSpec((tm, tn), lambda j, i: (i, j)),
        out_shape=jax.ShapeDtypeStruct((m, IN_WIDTH), F32),
        compiler_params=_params("parallel", "parallel"),
        name="in_proj",
    )(h, w_in)


def _softmax_pv(scores, values, extra_logit=None):
    m = scores[0].max(axis=-1, keepdims=True)
    for s in scores[1:]:
        m = jnp.maximum(m, s.max(axis=-1, keepdims=True))
    if extra_logit is not None:
        m = jnp.maximum(m, extra_logit)
    den = None
    acc = None
    for s, v in zip(scores, values):
        e = jnp.exp(s - m)
        d = e.sum(axis=-1, keepdims=True)
        a = _dot(e.astype(BF16), v)
        den = d if den is None else den + d
        acc = a if acc is None else acc + a
    if extra_logit is not None:
        den = den + jnp.exp(extra_logit - m)
    return acc / den


def _sink_column(sink_ref, base, rows_per_head):
    rows = lax.broadcasted_iota(jnp.int32, (REP_B * rows_per_head, 1), 0)
    col = jnp.full((REP_B * rows_per_head, 1), sink_ref[base], F32)
    for r in range(1, REP_B):
        col = jnp.where(rows >= r * rows_per_head, sink_ref[base + r], col)
    return col


def _gmlp(u_raw, v_raw, gw_ref, gb_ref, lng_ref, lnb_ref, write):
    u = jax.nn.gelu(u_raw)
    v = jax.nn.gelu(v_raw)
    mu = jnp.mean(v, axis=-1, keepdims=True)
    xc = v - mu
    vn = xc * lax.rsqrt(jnp.mean(xc * xc, axis=-1, keepdims=True) + NORM_EPS)
    vn = (vn * lng_ref[...] + lnb_ref[...]).astype(BF16)
    for n in range(u_raw.shape[0] // CHUNK):
        rs = slice(n * CHUNK, (n + 1) * CHUNK)
        for g in range(G_C):
            cs = slice(g * C_GROUP, (g + 1) * C_GROUP)
            s = _dot(gw_ref[g].astype(BF16), vn[rs, cs]) + gb_ref[g]
            write(n, g, u[rs, cs] * s)


def _ctx_mix_kernel(l, sink_ref, z_ref, gw_ref, gb_ref, lng_ref, lnb_ref,
                    ka_all, va_all, kb_all, vb_all, o_ref, ka_ref, va_ref, kb_ref, vb_ref):
    del ka_all, va_all, kb_all, vb_all

    def col(off, i=0):
        return z_ref[:, off + i * HEAD_DIM: off + (i + 1) * HEAD_DIM]

    for h in range(H_A):
        k = col(K_A, h)
        v = col(V_A, h)
        ka_ref[h] = k
        va_ref[h] = v
        s = _dot_nt(col(Q_A, h).astype(BF16), k.astype(BF16)) * ATTN_SCALE
        o_ref[:, h * HEAD_DIM:(h + 1) * HEAD_DIM] = _softmax_pv([s], [v.astype(BF16)]).astype(BF16)

    for g in range(KV_B):
        k = col(K_B, g)
        v = col(V_B, g)
        kb_ref[g] = k
        vb_ref[g] = v
        q = jnp.concatenate([col(Q_B, g * REP_B + r) for r in range(REP_B)], axis=0).astype(BF16)
        s = _dot_nt(q, k.astype(BF16)) * ATTN_SCALE
        sink = _sink_column(sink_ref, l * H_B + g * REP_B, SEQ)
        o = _softmax_pv([s], [v.astype(BF16)], sink).astype(BF16)
        for r in range(REP_B):
            c0 = W_A + (g * REP_B + r) * HEAD_DIM
            o_ref[:, c0:c0 + HEAD_DIM] = o[r * SEQ:(r + 1) * SEQ]

    def write(n, g, val):
        c0 = W_A + W_B + g * C_GROUP
        o_ref[n * CHUNK:(n + 1) * CHUNK, c0:c0 + C_GROUP] = val.astype(BF16)

    _gmlp(z_ref[:, U_C:V_C], z_ref[:, V_C:IN_WIDTH], gw_ref, gb_ref, lng_ref, lnb_ref, write)


def _gmlp_specs(l):
    return [
        pl.BlockSpec((None, G_C, CHUNK, CHUNK), lambda i, *_: (l, 0, 0, 0)),
        pl.BlockSpec((None, G_C, CHUNK, 1), lambda i, *_: (l, 0, 0, 0)),
        pl.BlockSpec((None, 1, W_C), lambda i, *_: (l, 0, 0)),
        pl.BlockSpec((None, 1, W_C), lambda i, *_: (l, 0, 0)),
    ]


def _ctx_mix(z, sink, gmlp_w, gmlp_b4, ln_g, ln_b, kv_all, l):
    kv_spec = lambda nh: pl.BlockSpec((None, None, nh, SEQ, HEAD_DIM), lambda b: (b, l, 0, 0, 0))
    n_in = 6
    return pl.pallas_call(
        functools.partial(_ctx_mix_kernel, l),
        grid=(BATCH,),
        in_specs=[
            pl.BlockSpec(memory_space=pltpu.SMEM),
            pl.BlockSpec((SEQ, IN_WIDTH), lambda b: (b, 0)),
        ] + _gmlp_specs(l) + [pl.BlockSpec(memory_space=pl.ANY)] * 4,
        out_specs=[pl.BlockSpec((SEQ, MIX_WIDTH), lambda b: (b, 0)),
                   kv_spec(H_A), kv_spec(H_A), kv_spec(KV_B), kv_spec(KV_B)],
        out_shape=[jax.ShapeDtypeStruct((N_CTX_TOK, MIX_WIDTH), BF16)]
                  + [jax.ShapeDtypeStruct(a.shape, a.dtype) for a in kv_all],
        input_output_aliases={n_in + i: 1 + i for i in range(4)},
        compiler_params=_params("parallel"),
        name="ctx_mix",
    )(sink, z, gmlp_w, gmlp_b4, ln_g, ln_b, *kv_all)


NA_QROWS = 4
NA_WROWS = 12
NA_QBLOCKS = GRID_ROWS // NA_QROWS


def _na_band_start(r):
    return min(max(r - NA_ROWS // 2, 0), GRID_ROWS - NA_ROWS)


def _na_window_start(qb):
    return min(max(qb * NA_QROWS - NA_ROWS // 2, 0), GRID_ROWS - NA_WROWS)


def _na_kernel(q_ref, k_ref, v_ref, ck_ref, cv_ref, bias_ref, o_ref):
    ck = ck_ref[...].astype(BF16)
    cv = cv_ref[...].astype(BF16)
    for qb in range(NA_QBLOCKS):
        w0 = _na_window_start(qb)
        win = slice(w0 * GRID_W, (w0 + NA_WROWS) * GRID_W)
        rows = slice(qb * NA_QROWS * GRID_W, (qb + 1) * NA_QROWS * GRID_W)
        q = q_ref[rows, :].astype(BF16)
        s_loc = _dot_nt(q, k_ref[win, :].astype(BF16)) * ATTN_SCALE + bias_ref[qb]
        s_ctx = _dot_nt(q, ck) * ATTN_SCALE
        o = _softmax_pv([s_loc, s_ctx], [v_ref[win, :].astype(BF16), cv])
        o_ref[rows, :] = o.astype(BF16)


def _na_bias(rpb):
    n_dc = 2 * NA_COLS - 1
    edge = GRID_W - NA_COLS
    ext = jnp.concatenate([jnp.repeat(rpb[..., :1], edge, axis=-1), rpb,
                           jnp.repeat(rpb[..., -1:], edge, axis=-1)], axis=-1).astype(F32)
    toep = jnp.stack([ext[..., GRID_W - 1 - q: 2 * GRID_W - 1 - q] for q in range(GRID_W)], axis=-2)
    col = np.arange(GRID_W)
    c0 = np.clip(col - NA_COLS // 2, 0, GRID_W - NA_COLS)
    col_ok = (col[None, :] >= c0[:, None]) & (col[None, :] < c0[:, None] + NA_COLS)
    toep = jnp.where(col_ok, toep, -jnp.inf)
    masked = jnp.full(toep.shape[:2] + (GRID_W, GRID_W), -jnp.inf, F32)
    blocks = []
    for qb in range(NA_QBLOCKS):
        w0 = _na_window_start(qb)
        q_rows = []
        for r in range(qb * NA_QROWS, (qb + 1) * NA_QROWS):
            r0 = _na_band_start(r)
            tiles = [toep[:, :, kr - r + NA_ROWS - 1] if r0 <= kr < r0 + NA_ROWS else masked
                     for kr in range(w0, w0 + NA_WROWS)]
            q_rows.append(jnp.concatenate(tiles, axis=-1))
        blocks.append(jnp.concatenate(q_rows, axis=-2))
    return jnp.stack(blocks, axis=2)


def _lat_na(z, cache_k, cache_v, bias, l):
    zc = lambda off: pl.BlockSpec((DEC_SEQ, HEAD_DIM), lambda h, b: (b, off // HEAD_DIM + h))
    cache = pl.BlockSpec((None, None, None, PAST_LEN, HEAD_DIM), lambda h, b: (b, l, h, 0, 0))
    return pl.pallas_call(
        _na_kernel,
        grid=(H_A, DEC_BATCH),
        in_specs=[zc(Q_A), zc(K_A), zc(V_A), cache, cache,
                  pl.BlockSpec((None, None, NA_QBLOCKS, NA_QROWS * GRID_W, NA_WROWS * GRID_W),
                               lambda h, b: (l, h, 0, 0, 0))],
        out_specs=pl.BlockSpec((DEC_SEQ, HEAD_DIM), lambda h, b: (b, h)),
        out_shape=jax.ShapeDtypeStruct((N_LAT_TOK, W_A), BF16),
        compiler_params=_params("parallel", "parallel"),
        name="lat_na",
    )(z, z, z, cache_k, cache_v, bias)


def _rope(x, cos, sin_signed):
    lane = lax.broadcasted_iota(jnp.int32, x.shape, 1)
    partner = jnp.where(lane % (HEAD_DIM // 2) < HEAD_DIM // 4,
                        pltpu.roll(x, HEAD_DIM - HEAD_DIM // 4, 1),
                        pltpu.roll(x, HEAD_DIM // 4, 1))
    return x * cos + partner * sin_signed


def _rope_tables():
    half = HEAD_DIM // 2
    quarter = half // 2
    t = np.arange(DEC_SEQ)
    freqs = ROPE_BASE ** (-jnp.arange(quarter, dtype=F32) / quarter)

    def tables(pos):
        ang = jnp.asarray(pos, F32)[:, None] * freqs[None, :]
        cos, sin = jnp.cos(ang), jnp.sin(ang)
        return jnp.concatenate([cos, cos], axis=-1), jnp.concatenate([-sin, sin], axis=-1)

    cr, sr = tables(t // GRID_W)
    cc, sc = tables(t % GRID_W)
    return jnp.concatenate([cr, cc], axis=-1), jnp.concatenate([sr, sc], axis=-1)


def _win_kernel(l, sink_ref, q_ref, k_ref, v_ref, ck_ref, cv_ref, cos_ref, sin_ref, o_ref, kr_ref):
    g = pl.program_id(1)
    ck = ck_ref[...].astype(BF16)
    cv = cv_ref[...].astype(BF16)
    kr_ref[...] = _rope(k_ref[...], cos_ref[...], sin_ref[...]).astype(BF16)
    sink = _sink_column(sink_ref, l * H_B + g * REP_B, BLK)
    n_blk = DEC_SEQ // BLK
    for nb in range(n_blk):
        rows = slice(nb * BLK, (nb + 1) * BLK)
        lo = max(nb - 1, 0) * BLK
        hi = min(nb + 2, n_blk) * BLK
        cos = cos_ref[rows, :]
        sin = sin_ref[rows, :]
        q = jnp.concatenate(
            [_rope(q_ref[rows, r * HEAD_DIM:(r + 1) * HEAD_DIM], cos, sin) for r in range(REP_B)],
            axis=0).astype(BF16)
        s_loc = _dot_nt(q, kr_ref[lo:hi, :]) * ATTN_SCALE
        qpos = nb * BLK + lax.broadcasted_iota(jnp.int32, s_loc.shape, 0) % BLK
        kpos = lo + lax.broadcasted_iota(jnp.int32, s_loc.shape, 1)
        s_loc = jnp.where(jnp.abs(qpos - kpos) <= WIN, s_loc, -jnp.inf)
        s_ctx = _dot_nt(q, ck) * ATTN_SCALE
        o = _softmax_pv([s_loc, s_ctx], [v_ref[lo:hi, :].astype(BF16), cv], sink).astype(BF16)
        for r in range(REP_B):
            o_ref[rows, r * HEAD_DIM:(r + 1) * HEAD_DIM] = o[r * BLK:(r + 1) * BLK]


def _lat_win(z, cache_k, cache_v, sink, cos, sin, l):
    zc = lambda off: pl.BlockSpec((DEC_SEQ, HEAD_DIM), lambda b, g: (b, off // HEAD_DIM + g))
    cache = pl.BlockSpec((None, None, None, PAST_LEN, HEAD_DIM), lambda b, g: (b, l, g, 0, 0))
    tab = pl.BlockSpec((DEC_SEQ, HEAD_DIM), lambda b, g: (0, 0))
    qw = REP_B * HEAD_DIM
    return pl.pallas_call(
        functools.partial(_win_kernel, l),
        grid=(DEC_BATCH, KV_B),
        in_specs=[pl.BlockSpec(memory_space=pltpu.SMEM),
                  pl.BlockSpec((DEC_SEQ, qw), lambda b, g: (b, Q_B // qw + g)),
                  zc(K_B), zc(V_B), cache, cache, tab, tab],
        out_specs=pl.BlockSpec((DEC_SEQ, qw), lambda b, g: (b, g)),
        out_shape=jax.ShapeDtypeStruct((N_LAT_TOK, W_B), BF16),
        scratch_shapes=[pltpu.VMEM((DEC_SEQ, HEAD_DIM), BF16)],
        compiler_params=_params("parallel", "parallel"),
        name="lat_win",
    )(sink, z, z, z, cache_k, cache_v, cos, sin)


def _gmlp_kernel(u_ref, v_ref, gw_ref, gb_ref, lng_ref, lnb_ref, o_ref):
    def write(n, g, val):
        o_ref[n * CHUNK:(n + 1) * CHUNK, g * C_GROUP:(g + 1) * C_GROUP] = val.astype(BF16)

    _gmlp(u_ref[...], v_ref[...], gw_ref, gb_ref, lng_ref, lnb_ref, write)


def _lat_gmlp(z, gmlp_w, gmlp_b4, ln_g, ln_b, l):
    rows = DEC_SEQ
    return pl.pallas_call(
        _gmlp_kernel,
        grid=(N_LAT_TOK // rows,),
        in_specs=[pl.BlockSpec((rows, W_C), lambda i: (i, U_C // W_C)),
                  pl.BlockSpec((rows, W_C), lambda i: (i, V_C // W_C))] + _gmlp_specs(l),
        out_specs=pl.BlockSpec((rows, W_C), lambda i: (i, 0)),
        out_shape=jax.ShapeDtypeStruct((N_LAT_TOK, W_C), BF16),
        compiler_params=_params("parallel"),
        name="lat_gmlp",
    )(z, z, gmlp_w, gmlp_b4, ln_g, ln_b)


EPI_ROWS = 128


def _residual_epilogue(y, rows, x_ref, g_post_ref, gate_ref, xo_ref, next_norm):
    x_new = x_ref[rows, :] + _rms(y, gate_ref[...] * g_post_ref[...])
    xo_ref[rows, :] = x_new
    if next_norm is not None:
        g_ref, sc_ref, sh_ref, h_ref = next_norm
        h_ref[rows, :] = (_rms(x_new, g_ref[...] * (1.0 + sc_ref[...])) + sh_ref[...]).astype(BF16)


def _out_kernel(widths, *refs):
    o_refs = refs[:len(widths)]
    (w_ref, x_ref, g_post_ref, gate_ref, g_next_ref, sc_ref, sh_ref,
     xo_ref, h_ref) = refs[len(widths):]
    for r in range(x_ref.shape[0] // EPI_ROWS):
        rows = slice(r * EPI_ROWS, (r + 1) * EPI_ROWS)
        y = None
        off = 0
        for o_ref, wd in zip(o_refs, widths):
            part = _dot(o_ref[rows, :], w_ref[off:off + wd, :])
            y = part if y is None else y + part
            off += wd
        _residual_epilogue(y, rows, x_ref, g_post_ref, gate_ref, xo_ref,
                           (g_next_ref, sc_ref, sh_ref, h_ref))


def _out_proj(o_list, x, mod5, g_post, g_ffn_pre, w_out, l, row_fn, tm):
    m = x.shape[0]
    widths = tuple(o.shape[1] for o in o_list)
    row_spec = pl.BlockSpec((tm, D_MODEL), lambda i: (i, 0))
    return pl.pallas_call(
        functools.partial(_out_kernel, widths),
        grid=(m // tm,),
        in_specs=[pl.BlockSpec((tm, wd), lambda i: (i, 0)) for wd in widths] + [
            pl.BlockSpec((None, MIX_WIDTH, D_MODEL), lambda i: (l, 0, 0)),
            row_spec,
            _vec_spec(l),
            _mod_spec(l, 2, row_fn),
            _vec_spec(l),
            _mod_spec(l, 4, row_fn),
            _mod_spec(l, 3, row_fn),
        ],
        out_specs=[row_spec, row_spec],
        out_shape=[jax.ShapeDtypeStruct((m, D_MODEL), F32),
                   jax.ShapeDtypeStruct((m, D_MODEL), BF16)],
        compiler_params=_params("parallel"),
        name="out_proj",
    )(*o_list, w_out, x, g_post, mod5, g_ffn_pre, mod5, mod5)


FFN_TC = 2 * MXU_TILE
FFN_ROWS = 512


def _ffn_kernel(with_next, h_ref, wg_ref, wu_ref, wd_ref, x_ref, g_post_ref, gate_ref, *refs):
    if with_next:
        g_next_ref, sc_ref, sh_ref, xo_ref, ho_ref = refs
        next_norm = (g_next_ref, sc_ref, sh_ref, ho_ref)
    else:
        (xo_ref,) = refs
        next_norm = None
    c = pl.program_id(1)
    last = pl.num_programs(1) - 1

    def chunk_product(rows):
        h = h_ref[rows, :]
        a = (jax.nn.silu(_dot(h, wg_ref[...])) * _dot(h, wu_ref[...])).astype(BF16)
        return _dot(a, wd_ref[...])

    def row_blocks():
        for r in range(x_ref.shape[0] // FFN_ROWS):
            yield slice(r * FFN_ROWS, (r + 1) * FFN_ROWS)

    @pl.when(c == 0)
    def _():
        for rows in row_blocks():
            xo_ref[rows, :] = chunk_product(rows)

    @pl.when((c > 0) & (c < last))
    def _():
        for rows in row_blocks():
            xo_ref[rows, :] += chunk_product(rows)

    @pl.when(c == last)
    def _():
        for rows in row_blocks():
            y = xo_ref[rows, :] + chunk_product(rows)
            _residual_epilogue(y, rows, x_ref, g_post_ref, gate_ref, xo_ref, next_norm)


def _ffn(h, x, mod5, g_post, g_mix_pre, w_gate, w_up, w_down, l, row_fn, tm):
    m = x.shape[0]
    with_next = l + 1 < DEPTH
    row_spec = pl.BlockSpec((tm, D_MODEL), lambda i, c: (i, 0))
    up_spec = pl.BlockSpec((None, D_MODEL, FFN_TC), lambda i, c: (l, 0, c))
    in_specs = [
        row_spec, up_spec, up_spec,
        pl.BlockSpec((None, FFN_TC, D_MODEL), lambda i, c: (l, c, 0)),
        pl.BlockSpec((tm, D_MODEL), lambda i, c: (i, 0), pipeline_mode=pl.Buffered(1)),
        _vec_spec(l),
        _mod_spec(l, 5, row_fn),
    ]
    args = [h, w_gate, w_up, w_down, x, g_post, mod5]
    out_specs = [row_spec]
    out_shape = [jax.ShapeDtypeStruct((m, D_MODEL), F32)]
    if with_next:
        in_specs += [_vec_spec(l + 1), _mod_spec(l + 1, 1, row_fn), _mod_spec(l + 1, 0, row_fn)]
        args += [g_mix_pre, mod5, mod5]
        out_specs.append(row_spec)
        out_shape.append(jax.ShapeDtypeStruct((m, D_MODEL), BF16))
    outs = pl.pallas_call(
        functools.partial(_ffn_kernel, with_next),
        grid=(m // tm, D_FF // FFN_TC),
        in_specs=in_specs,
        out_specs=out_specs,
        out_shape=out_shape,
        compiler_params=_params("parallel", "arbitrary"),
        name="ffn",
    )(*args)
    return outs if with_next else (outs[0], None)


def kernel(x_prompt, x_sample, cache_a_k, cache_a_v, cache_b_k, cache_b_v, c, c_ctx,
           mod_w, mod_b, norm_mix_pre, norm_mix_post, norm_ffn_pre, norm_ffn_post,
           w_in, w_out, rpb_a, sink_b, gmlp_ln_g, gmlp_ln_b, gmlp_w, gmlp_b,
           w_gate, w_up, w_down):
    cond = jnp.zeros((MOD_ROWS, D_MODEL), F32)
    cond = cond.at[:DEC_BATCH].set(c).at[CTX_MOD_ROW].set(c_ctx)
    mod5 = _modulation(cond, mod_w, mod_b).reshape(DEPTH, MOD_ROWS, N_MOD, 1, D_MODEL)

    vec = lambda a: a.reshape(DEPTH, 1, -1)
    g_mix_pre, g_mix_post = vec(norm_mix_pre), vec(norm_mix_post)
    g_ffn_pre, g_ffn_post = vec(norm_ffn_pre), vec(norm_ffn_post)
    ln_g, ln_b = vec(gmlp_ln_g), vec(gmlp_ln_b)
    gmlp_b4 = gmlp_b.reshape(DEPTH, G_C, CHUNK, 1)
    sink = sink_b.reshape(DEPTH * H_B)
    w_in_b, w_out_b = w_in.astype(BF16), w_out.astype(BF16)
    w_gate_b, w_up_b, w_down_b = w_gate.astype(BF16), w_up.astype(BF16), w_down.astype(BF16)
    bias = _na_bias(rpb_a)
    cos, sin = _rope_tables()

    tm_wide, tm_row = 1024, 512
    ctx_row = lambda i: CTX_MOD_ROW
    lat_row = lambda tm: (lambda i: (i * tm) // DEC_SEQ)

    xp = x_prompt.reshape(N_CTX_TOK, D_MODEL)
    xs = x_sample.reshape(N_LAT_TOK, D_MODEL)
    hp = _norm_mod(xp, mod5, g_mix_pre, 0, ctx_row, tm_row)
    hs = _norm_mod(xs, mod5, g_mix_pre, 0, lat_row(tm_row), tm_row)
    kv_all = [jnp.zeros((BATCH, DEPTH, nh, SEQ, HEAD_DIM), F32) for nh in (H_A, H_A, KV_B, KV_B)]
    for l in range(DEPTH):
        z = _in_proj(hp, w_in_b, l, tm_wide)
        o, *kv_all = _ctx_mix(z, sink, gmlp_w, gmlp_b4, ln_g, ln_b, kv_all, l)
        xp, hp = _out_proj([o], xp, mod5, g_mix_post, g_ffn_pre, w_out_b, l, ctx_row, tm_row)
        xp, hp = _ffn(hp, xp, mod5, g_ffn_post, g_mix_pre, w_gate_b, w_up_b, w_down_b, l,
                      ctx_row, tm_wide)
        z = _in_proj(hs, w_in_b, l, tm_wide)
        o_a = _lat_na(z, cache_a_k, cache_a_v, bias, l)
        o_b = _lat_win(z, cache_b_k, cache_b_v, sink, cos, sin, l)
        o_c = _lat_gmlp(z, gmlp_w, gmlp_b4, ln_g, ln_b, l)
        xs, hs = _out_proj([o_a, o_b, o_c], xs, mod5, g_mix_post, g_ffn_pre, w_out_b, l,
                           lat_row(tm_row), tm_row)
        xs, hs = _ffn(hs, xs, mod5, g_ffn_post, g_mix_pre, w_gate_b, w_up_b, w_down_b, l,
                      lat_row(tm_wide), tm_wide)

    return (xp.reshape(BATCH, SEQ, D_MODEL), xs.reshape(DEC_BATCH, DEC_SEQ, D_MODEL), *kv_all)
```

```python
import functools

import jax
import jax.numpy as jnp
import numpy as np
from jax import lax
from jax.experimental import pallas as pl
from jax.experimental.pallas import tpu as pltpu

D_MODEL = 2048
BATCH = 32
SEQ = 256
DEPTH = 4
DEC_BATCH = 8
DEC_SEQ = 1024
PAST_LEN = 256
GRID_W = 64
GRID_ROWS = DEC_SEQ // GRID_W
HEAD_DIM = 128
H_A = 6
H_B = 6
KV_B = 2
REP_B = H_B // KV_B
G_C = 4
C_GROUP = 128
W_A = H_A * HEAD_DIM
W_B = H_B * HEAD_DIM
W_C = G_C * C_GROUP
MIX_WIDTH = W_A + W_B + W_C
NA_ROWS = 8
NA_COLS = 16
WIN = 128
BLK = 128
CHUNK = 128
D_FF = 5632
N_MOD = 6
ROPE_BASE = 10000.0
NORM_EPS = 1e-6
ATTN_SCALE = HEAD_DIM ** -0.5
Q_A = 0
K_A = Q_A + W_A
V_A = K_A + W_A
Q_B = V_A + W_A
K_B = Q_B + W_B
V_B = K_B + KV_B * HEAD_DIM
U_C = V_B + KV_B * HEAD_DIM
V_C = U_C + W_C
IN_WIDTH = V_C + W_C

N_CTX_TOK = BATCH * SEQ
N_LAT_TOK = DEC_BATCH * DEC_SEQ
MOD_ROWS = 16
CTX_MOD_ROW = DEC_BATCH

VMEM_LIMIT_BYTES = 58 * 1024 * 1024
MXU_TILE = 256

BF16 = jnp.bfloat16
F32 = jnp.float32


def _params(*sem):
    return pltpu.CompilerParams(dimension_semantics=sem, vmem_limit_bytes=VMEM_LIMIT_BYTES)


def _dot(a, b):
    return jnp.dot(a, b, preferred_element_type=F32)


def _dot_nt(a, b):
    return lax.dot_general(a, b, (((1,), (1,)), ((), ())), preferred_element_type=F32)


def _rms(x, g):
    return (x * lax.rsqrt(jnp.mean(x * x, axis=-1, keepdims=True) + NORM_EPS)) * g


def _mod_kernel(c_ref, w_ref, b_ref, o_ref):
    s = jax.nn.silu(c_ref[...])
    o_ref[...] = _dot(s.astype(BF16), w_ref[...].astype(BF16)) + b_ref[...]


def _modulation(cond, mod_w, mod_b):
    tn = 1024
    n = N_MOD * D_MODEL
    return pl.pallas_call(
        _mod_kernel,
        grid=(DEPTH, n // tn),
        in_specs=[
            pl.BlockSpec((MOD_ROWS, D_MODEL), lambda l, j: (0, 0)),
            pl.BlockSpec((None, D_MODEL, tn), lambda l, j: (l, 0, j)),
            pl.BlockSpec((None, 1, tn), lambda l, j: (l, 0, j)),
        ],
        out_specs=pl.BlockSpec((None, MOD_ROWS, tn), lambda l, j: (l, 0, j)),
        out_shape=jax.ShapeDtypeStruct((DEPTH, MOD_ROWS, n), F32),
        compiler_params=_params("parallel", "parallel"),
        name="modulation",
    )(cond, mod_w, mod_b.reshape(DEPTH, 1, n))


def _mod_spec(l, which, row_fn):
    return pl.BlockSpec((None, None, None, 1, D_MODEL),
                        lambda i, *_: (l, row_fn(i), which, 0, 0))


def _vec_spec(l):
    return pl.BlockSpec((None, 1, D_MODEL), lambda i, *_: (l, 0, 0))


def _modulated_norm(x, g, scale, shift):
    return (_rms(x, g) * (1.0 + scale) + shift).astype(BF16)


def _norm_kernel(x_ref, g_ref, sc_ref, sh_ref, h_ref):
    h_ref[...] = _modulated_norm(x_ref[...], g_ref[...], sc_ref[...], sh_ref[...])


def _norm_mod(x, mod5, g_pre, l, row_fn, tm):
    m = x.shape[0]
    return pl.pallas_call(
        _norm_kernel,
        grid=(m // tm,),
        in_specs=[pl.BlockSpec((tm, D_MODEL), lambda i: (i, 0)),
                  _vec_spec(l), _mod_spec(l, 1, row_fn), _mod_spec(l, 0, row_fn)],
        out_specs=pl.BlockSpec((tm, D_MODEL), lambda i: (i, 0)),
        out_shape=jax.ShapeDtypeStruct((m, D_MODEL), BF16),
        compiler_params=_params("parallel"),
        name="norm_mod",
    )(x, g_pre, mod5, mod5)


def _in_kernel(h_ref, w_ref, z_ref):
    z_ref[...] = _dot(h_ref[...], w_ref[...])


def _in_proj(h, w_in, l, tm):
    m = h.shape[0]
    tn = 6 * MXU_TILE
    return pl.pallas_call(
        _in_kernel,
        grid=(IN_WIDTH // tn, m // tm),
        in_specs=[
            pl.BlockSpec((tm, D_MODEL), lambda j, i: (i, 0)),
            pl.BlockSpec((None, D_MODEL, tn), lambda j, i: (l, 0, j)),
        ],
        out_specs=pl.BlockSpec((tm, tn), lambda j, i: (i, j)),
        out_shape=jax.ShapeDtypeStruct((m, IN_WIDTH), F32),
        compiler_params=_params("parallel", "parallel"),
        name="in_proj",
    )(h, w_in)


def _softmax_pv(scores, values, extra_logit=None):
    m = scores[0].max(axis=-1, keepdims=True)
    for s in scores[1:]:
        m = jnp.maximum(m, s.max(axis=-1, keepdims=True))
    if extra_logit is not None:
        m = jnp.maximum(m, extra_logit)
    den = None
    acc = None
    for s, v in zip(scores, values):
        e = jnp.exp(s - m)
        d = e.sum(axis=-1, keepdims=True)
        a = _dot(e.astype(BF16), v)
        den = d if den is None else den + d
        acc = a if acc is None else acc + a
    if extra_logit is not None:
        den = den + jnp.exp(extra_logit - m)
    return acc / den


def _sink_column(sink_ref, base, rows_per_head):
    rows = lax.broadcasted_iota(jnp.int32, (REP_B * rows_per_head, 1), 0)
    col = jnp.full((REP_B * rows_per_head, 1), sink_ref[base], F32)
    for r in range(1, REP_B):
        col = jnp.where(rows >= r * rows_per_head, sink_ref[base + r], col)
    return col


def _gmlp(u_raw, v_raw, gw_ref, gb_ref, lng_ref, lnb_ref, write):
    u = jax.nn.gelu(u_raw)
    v = jax.nn.gelu(v_raw)
    mu = jnp.mean(v, axis=-1, keepdims=True)
    xc = v - mu
    vn = xc * lax.rsqrt(jnp.mean(xc * xc, axis=-1, keepdims=True) + NORM_EPS)
    vn = (vn * lng_ref[...] + lnb_ref[...]).astype(BF16)
    for n in range(u_raw.shape[0] // CHUNK):
        rs = slice(n * CHUNK, (n + 1) * CHUNK)
        for g in range(G_C):
            cs = slice(g * C_GROUP, (g + 1) * C_GROUP)
            s = _dot(gw_ref[g].astype(BF16), vn[rs, cs]) + gb_ref[g]
            write(n, g, u[rs, cs] * s)


CTX_PROJ_COLS = 2 * MXU_TILE


def _ctx_mix_kernel(l, sink_ref, h_ref, w_ref, gw_ref, gb_ref, lng_ref, lnb_ref,
                    ka_all, va_all, kb_all, vb_all, o_ref, ka_ref, va_ref, kb_ref, vb_ref, z_ref):
    del ka_all, va_all, kb_all, vb_all
    h_in = h_ref[...]
    for c0 in range(0, IN_WIDTH, CTX_PROJ_COLS):
        z_ref[:, c0:c0 + CTX_PROJ_COLS] = _dot(h_in, w_ref[:, c0:c0 + CTX_PROJ_COLS])

    def col(off, i=0):
        return z_ref[:, off + i * HEAD_DIM: off + (i + 1) * HEAD_DIM]

    for h in range(H_A):
        k = col(K_A, h)
        v = col(V_A, h)
        ka_ref[h] = k
        va_ref[h] = v
        s = _dot_nt(col(Q_A, h).astype(BF16), k.astype(BF16)) * ATTN_SCALE
        o_ref[:, h * HEAD_DIM:(h + 1) * HEAD_DIM] = _softmax_pv([s], [v.astype(BF16)]).astype(BF16)

    for g in range(KV_B):
        k = col(K_B, g)
        v = col(V_B, g)
        kb_ref[g] = k
        vb_ref[g] = v
        q = jnp.concatenate([col(Q_B, g * REP_B + r) for r in range(REP_B)], axis=0).astype(BF16)
        s = _dot_nt(q, k.astype(BF16)) * ATTN_SCALE
        sink = _sink_column(sink_ref, l * H_B + g * REP_B, SEQ)
        o = _softmax_pv([s], [v.astype(BF16)], sink).astype(BF16)
        for r in range(REP_B):
            c0 = W_A + (g * REP_B + r) * HEAD_DIM
            o_ref[:, c0:c0 + HEAD_DIM] = o[r * SEQ:(r + 1) * SEQ]

    def write(n, g, val):
        c0 = W_A + W_B + g * C_GROUP
        o_ref[n * CHUNK:(n + 1) * CHUNK, c0:c0 + C_GROUP] = val.astype(BF16)

    _gmlp(z_ref[:, U_C:V_C], z_ref[:, V_C:IN_WIDTH], gw_ref, gb_ref, lng_ref, lnb_ref, write)


def _gmlp_specs(l):
    return [
        pl.BlockSpec((None, G_C, CHUNK, CHUNK), lambda i, *_: (l, 0, 0, 0)),
        pl.BlockSpec((None, G_C, CHUNK, 1), lambda i, *_: (l, 0, 0, 0)),
        pl.BlockSpec((None, 1, W_C), lambda i, *_: (l, 0, 0)),
        pl.BlockSpec((None, 1, W_C), lambda i, *_: (l, 0, 0)),
    ]


def _ctx_mix(h, w_in, sink, gmlp_w, gmlp_b4, ln_g, ln_b, kv_all, l):
    kv_spec = lambda nh: pl.BlockSpec((None, None, nh, SEQ, HEAD_DIM), lambda b: (b, l, 0, 0, 0))
    n_in = 7
    return pl.pallas_call(
        functools.partial(_ctx_mix_kernel, l),
        grid=(BATCH,),
        in_specs=[
            pl.BlockSpec(memory_space=pltpu.SMEM),
            pl.BlockSpec((SEQ, D_MODEL), lambda b: (b, 0)),
            pl.BlockSpec((None, D_MODEL, IN_WIDTH), lambda b: (l, 0, 0), pipeline_mode=pl.Buffered(1)),
        ] + _gmlp_specs(l) + [pl.BlockSpec(memory_space=pl.ANY)] * 4,
        out_specs=[pl.BlockSpec((SEQ, MIX_WIDTH), lambda b: (b, 0)),
                   kv_spec(H_A), kv_spec(H_A), kv_spec(KV_B), kv_spec(KV_B)],
        out_shape=[jax.ShapeDtypeStruct((N_CTX_TOK, MIX_WIDTH), BF16)]
                  + [jax.ShapeDtypeStruct(a.shape, a.dtype) for a in kv_all],
        scratch_shapes=[pltpu.VMEM((SEQ, IN_WIDTH), F32)],
        input_output_aliases={n_in + i: 1 + i for i in range(4)},
        compiler_params=_params("parallel"),
        name="ctx_mix",
    )(sink, h, w_in, gmlp_w, gmlp_b4, ln_g, ln_b, *kv_all)


NA_QROWS = 4
NA_WROWS = 12
NA_QBLOCKS = GRID_ROWS // NA_QROWS


def _na_band_start(r):
    return min(max(r - NA_ROWS // 2, 0), GRID_ROWS - NA_ROWS)


def _na_window_start(qb):
    return min(max(qb * NA_QROWS - NA_ROWS // 2, 0), GRID_ROWS - NA_WROWS)


def _na_kernel(q_ref, k_ref, v_ref, ck_ref, cv_ref, bias_ref, o_ref):
    ck = ck_ref[...].astype(BF16)
    cv = cv_ref[...].astype(BF16)
    for qb in range(NA_QBLOCKS):
        w0 = _na_window_start(qb)
        win = slice(w0 * GRID_W, (w0 + NA_WROWS) * GRID_W)
        rows = slice(qb * NA_QROWS * GRID_W, (qb + 1) * NA_QROWS * GRID_W)
        q = q_ref[rows, :].astype(BF16)
        s_loc = _dot_nt(q, k_ref[win, :].astype(BF16)) * ATTN_SCALE + bias_ref[qb]
        s_ctx = _dot_nt(q, ck) * ATTN_SCALE
        o = _softmax_pv([s_loc, s_ctx], [v_ref[win, :].astype(BF16), cv])
        o_ref[rows, :] = o.astype(BF16)


def _na_bias(rpb):
    n_dc = 2 * NA_COLS - 1
    edge = GRID_W - NA_COLS
    ext = jnp.concatenate([jnp.repeat(rpb[..., :1], edge, axis=-1), rpb,
                           jnp.repeat(rpb[..., -1:], edge, axis=-1)], axis=-1).astype(F32)
    toep = jnp.stack([ext[..., GRID_W - 1 - q: 2 * GRID_W - 1 - q] for q in range(GRID_W)], axis=-2)
    col = np.arange(GRID_W)
    c0 = np.clip(col - NA_COLS // 2, 0, GRID_W - NA_COLS)
    col_ok = (col[None, :] >= c0[:, None]) & (col[None, :] < c0[:, None] + NA_COLS)
    toep = jnp.where(col_ok, toep, -jnp.inf)
    masked = jnp.full(toep.shape[:2] + (GRID_W, GRID_W), -jnp.inf, F32)
    blocks = []
    for qb in range(NA_QBLOCKS):
        w0 = _na_window_start(qb)
        q_rows = []
        for r in range(qb * NA_QROWS, (qb + 1) * NA_QROWS):
            r0 = _na_band_start(r)
            tiles = [toep[:, :, kr - r + NA_ROWS - 1] if r0 <= kr < r0 + NA_ROWS else masked
                     for kr in range(w0, w0 + NA_WROWS)]
            q_rows.append(jnp.concatenate(tiles, axis=-1))
        blocks.append(jnp.concatenate(q_rows, axis=-2))
    return jnp.stack(blocks, axis=2)


def _lat_na(z, cache_k, cache_v, bias, l):
    zc = lambda off: pl.BlockSpec((DEC_SEQ, HEAD_DIM), lambda h, b: (b, off // HEAD_DIM + h))
    cache = pl.BlockSpec((None, None, None, PAST_LEN, HEAD_DIM), lambda h, b: (b, l, h, 0, 0))
    return pl.pallas_call(
        _na_kernel,
        grid=(H_A, DEC_BATCH),
        in_specs=[zc(Q_A), zc(K_A), zc(V_A), cache, cache,
                  pl.BlockSpec((None, None, NA_QBLOCKS, NA_QROWS * GRID_W, NA_WROWS * GRID_W),
                               lambda h, b: (l, h, 0, 0, 0))],
        out_specs=pl.BlockSpec((DEC_SEQ, HEAD_DIM), lambda h, b: (b, h)),
        out_shape=jax.ShapeDtypeStruct((N_LAT_TOK, W_A), BF16),
        compiler_params=_params("parallel", "parallel"),
        name="lat_na",
    )(z, z, z, cache_k, cache_v, bias)


def _rope(x, cos, sin_signed):
    lane = lax.broadcasted_iota(jnp.int32, x.shape, 1)
    partner = jnp.where(lane % (HEAD_DIM // 2) < HEAD_DIM // 4,
                        pltpu.roll(x, HEAD_DIM - HEAD_DIM // 4, 1),
                        pltpu.roll(x, HEAD_DIM // 4, 1))
    return x * cos + partner * sin_signed


def _rope_tables():
    half = HEAD_DIM // 2
    quarter = half // 2
    t = np.arange(DEC_SEQ)
    freqs = ROPE_BASE ** (-jnp.arange(quarter, dtype=F32) / quarter)

    def tables(pos):
        ang = jnp.asarray(pos, F32)[:, None] * freqs[None, :]
        cos, sin = jnp.cos(ang), jnp.sin(ang)
        return jnp.concatenate([cos, cos], axis=-1), jnp.concatenate([-sin, sin], axis=-1)

    cr, sr = tables(t // GRID_W)
    cc, sc = tables(t % GRID_W)
    return jnp.concatenate([cr, cc], axis=-1), jnp.concatenate([sr, sc], axis=-1)


def _win_kernel(l, sink_ref, q_ref, k_ref, v_ref, ck_ref, cv_ref, cos_ref, sin_ref, o_ref, kr_ref):
    g = pl.program_id(1)
    ck = ck_ref[...].astype(BF16)
    cv = cv_ref[...].astype(BF16)
    kr_ref[...] = _rope(k_ref[...], cos_ref[...], sin_ref[...]).astype(BF16)
    sink = _sink_column(sink_ref, l * H_B + g * REP_B, BLK)
    n_blk = DEC_SEQ // BLK
    span = BLK + 2 * WIN

    def window_bias(first_key_offset):
        qpos = first_key_offset + lax.broadcasted_iota(jnp.int32, (REP_B * BLK, span), 0) % BLK
        kpos = lax.broadcasted_iota(jnp.int32, (REP_B * BLK, span), 1)
        return jnp.where(jnp.abs(qpos - kpos) <= WIN, 0.0, -jnp.inf).astype(F32)

    bias_first = window_bias(0)
    bias_rest = window_bias(BLK)
    for nb in range(n_blk):
        rows = slice(nb * BLK, (nb + 1) * BLK)
        lo = max(nb - 1, 0) * BLK
        hi = min(nb + 2, n_blk) * BLK
        cos = cos_ref[rows, :]
        sin = sin_ref[rows, :]
        q = jnp.concatenate(
            [_rope(q_ref[rows, r * HEAD_DIM:(r + 1) * HEAD_DIM], cos, sin) for r in range(REP_B)],
            axis=0).astype(BF16)
        bias = (bias_first if nb == 0 else bias_rest)[:, :hi - lo]
        s_loc = _dot_nt(q, kr_ref[lo:hi, :]) * ATTN_SCALE + bias
        s_ctx = _dot_nt(q, ck) * ATTN_SCALE
        o = _softmax_pv([s_loc, s_ctx], [v_ref[lo:hi, :].astype(BF16), cv], sink).astype(BF16)
        for r in range(REP_B):
            o_ref[rows, r * HEAD_DIM:(r + 1) * HEAD_DIM] = o[r * BLK:(r + 1) * BLK]


def _lat_win(z, cache_k, cache_v, sink, cos, sin, l):
    zc = lambda off: pl.BlockSpec((DEC_SEQ, HEAD_DIM), lambda b, g: (b, off // HEAD_DIM + g))
    cache = pl.BlockSpec((None, None, None, PAST_LEN, HEAD_DIM), lambda b, g: (b, l, g, 0, 0))
    tab = pl.BlockSpec((DEC_SEQ, HEAD_DIM), lambda b, g: (0, 0))
    qw = REP_B * HEAD_DIM
    return pl.pallas_call(
        functools.partial(_win_kernel, l),
        grid=(DEC_BATCH, KV_B),
        in_specs=[pl.BlockSpec(memory_space=pltpu.SMEM),
                  pl.BlockSpec((DEC_SEQ, qw), lambda b, g: (b, Q_B // qw + g)),
                  zc(K_B), zc(V_B), cache, cache, tab, tab],
        out_specs=pl.BlockSpec((DEC_SEQ, qw), lambda b, g: (b, g)),
        out_shape=jax.ShapeDtypeStruct((N_LAT_TOK, W_B), BF16),
        scratch_shapes=[pltpu.VMEM((DEC_SEQ, HEAD_DIM), BF16)],
        compiler_params=_params("parallel", "parallel"),
        name="lat_win",
    )(sink, z, z, z, cache_k, cache_v, cos, sin)


def _gmlp_kernel(u_ref, v_ref, gw_ref, gb_ref, lng_ref, lnb_ref, o_ref):
    def write(n, g, val):
        o_ref[n * CHUNK:(n + 1) * CHUNK, g * C_GROUP:(g + 1) * C_GROUP] = val.astype(BF16)

    _gmlp(u_ref[...], v_ref[...], gw_ref, gb_ref, lng_ref, lnb_ref, write)


def _lat_gmlp(z, gmlp_w, gmlp_b4, ln_g, ln_b, l):
    rows = DEC_SEQ
    return pl.pallas_call(
        _gmlp_kernel,
        grid=(N_LAT_TOK // rows,),
        in_specs=[pl.BlockSpec((rows, W_C), lambda i: (i, U_C // W_C)),
                  pl.BlockSpec((rows, W_C), lambda i: (i, V_C // W_C))] + _gmlp_specs(l),
        out_specs=pl.BlockSpec((rows, W_C), lambda i: (i, 0)),
        out_shape=jax.ShapeDtypeStruct((N_LAT_TOK, W_C), BF16),
        compiler_params=_params("parallel"),
        name="lat_gmlp",
    )(z, z, gmlp_w, gmlp_b4, ln_g, ln_b)


EPI_ROWS = 128


def _residual_epilogue(y, rows, x_ref, g_post_ref, gate_ref, xo_ref, next_norm):
    x_new = x_ref[rows, :] + _rms(y, gate_ref[...] * g_post_ref[...])
    xo_ref[rows, :] = x_new
    if next_norm is not None:
        g_ref, sc_ref, sh_ref, h_ref = next_norm
        h_ref[rows, :] = (_rms(x_new, g_ref[...] * (1.0 + sc_ref[...])) + sh_ref[...]).astype(BF16)


def _out_kernel(widths, *refs):
    o_refs = refs[:len(widths)]
    (w_ref, x_ref, g_post_ref, gate_ref, g_next_ref, sc_ref, sh_ref,
     xo_ref, h_ref) = refs[len(widths):]
    for r in range(x_ref.shape[0] // EPI_ROWS):
        rows = slice(r * EPI_ROWS, (r + 1) * EPI_ROWS)
        y = None
        off = 0
        for o_ref, wd in zip(o_refs, widths):
            part = _dot(o_ref[rows, :], w_ref[off:off + wd, :])
            y = part if y is None else y + part
            off += wd
        _residual_epilogue(y, rows, x_ref, g_post_ref, gate_ref, xo_ref,
                           (g_next_ref, sc_ref, sh_ref, h_ref))


def _out_proj(o_list, x, mod5, g_post, g_ffn_pre, w_out, l, row_fn, tm):
    m = x.shape[0]
    widths = tuple(o.shape[1] for o in o_list)
    row_spec = pl.BlockSpec((tm, D_MODEL), lambda i: (i, 0))
    return pl.pallas_call(
        functools.partial(_out_kernel, widths),
        grid=(m // tm,),
        in_specs=[pl.BlockSpec((tm, wd), lambda i: (i, 0)) for wd in widths] + [
            pl.BlockSpec((None, MIX_WIDTH, D_MODEL), lambda i: (l, 0, 0)),
            row_spec,
            _vec_spec(l),
            _mod_spec(l, 2, row_fn),
            _vec_spec(l),
            _mod_spec(l, 4, row_fn),
            _mod_spec(l, 3, row_fn),
        ],
        out_specs=[row_spec, row_spec],
        out_shape=[jax.ShapeDtypeStruct((m, D_MODEL), F32),
                   jax.ShapeDtypeStruct((m, D_MODEL), BF16)],
        compiler_params=_params("parallel"),
        name="out_proj",
    )(*o_list, w_out, x, g_post, mod5, g_ffn_pre, mod5, mod5)


FFN_TC = 2 * MXU_TILE
FFN_ROWS = 512
FFN_LAST_ROWS = 256


def _ffn_kernel(with_next, h_ref, wg_ref, wu_ref, wd_ref, x_hbm, g_post_ref, gate_ref, *refs):
    if with_next:
        g_next_ref, sc_ref, sh_ref, xo_ref, ho_ref, x_buf, x_sem = refs
        next_norm = (g_next_ref, sc_ref, sh_ref, ho_ref)
    else:
        xo_ref, x_buf, x_sem = refs
        next_norm = None
    tm = x_buf.shape[0]
    c = pl.program_id(1)
    last = pl.num_programs(1) - 1

    def x_copy():
        row0 = pl.multiple_of(pl.program_id(0) * tm, tm)
        return pltpu.make_async_copy(x_hbm.at[pl.ds(row0, tm), :], x_buf, x_sem)

    def chunk_product(rows):
        h = h_ref[rows, :]
        a = (jax.nn.silu(_dot(h, wg_ref[...])) * _dot(h, wu_ref[...])).astype(BF16)
        return _dot(a, wd_ref[...])

    def row_blocks(n):
        for r in range(tm // n):
            yield slice(r * n, (r + 1) * n)

    @pl.when(c == 0)
    def _():
        x_copy().start()
        for rows in row_blocks(FFN_ROWS):
            xo_ref[rows, :] = chunk_product(rows)

    @pl.when((c > 0) & (c < last))
    def _():
        for rows in row_blocks(FFN_ROWS):
            xo_ref[rows, :] += chunk_product(rows)

    @pl.when(c == last)
    def _():
        x_copy().wait()
        for rows in row_blocks(FFN_LAST_ROWS):
            y = xo_ref[rows, :] + chunk_product(rows)
            _residual_epilogue(y, rows, x_buf, g_post_ref, gate_ref, xo_ref, next_norm)


def _ffn(h, x, mod5, g_post, g_mix_pre, w_gate, w_up, w_down, l, row_fn, tm):
    m = x.shape[0]
    with_next = l + 1 < DEPTH
    row_spec = pl.BlockSpec((tm, D_MODEL), lambda i, c: (i, 0))
    up_spec = pl.BlockSpec((None, D_MODEL, FFN_TC), lambda i, c: (l, 0, c))
    in_specs = [
        row_spec, up_spec, up_spec,
        pl.BlockSpec((None, FFN_TC, D_MODEL), lambda i, c: (l, c, 0)),
        pl.BlockSpec(memory_space=pl.ANY),
        _vec_spec(l),
        _mod_spec(l, 5, row_fn),
    ]
    args = [h, w_gate, w_up, w_down, x, g_post, mod5]
    out_specs = [row_spec]
    out_shape = [jax.ShapeDtypeStruct((m, D_MODEL), F32)]
    if with_next:
        in_specs += [_vec_spec(l + 1), _mod_spec(l + 1, 1, row_fn), _mod_spec(l + 1, 0, row_fn)]
        args += [g_mix_pre, mod5, mod5]
        out_specs.append(row_spec)
        out_shape.append(jax.ShapeDtypeStruct((m, D_MODEL), BF16))
    outs = pl.pallas_call(
        functools.partial(_ffn_kernel, with_next),
        grid=(m // tm, D_FF // FFN_TC),
        in_specs=in_specs,
        out_specs=out_specs,
        out_shape=out_shape,
        scratch_shapes=[pltpu.VMEM((tm, D_MODEL), F32), pltpu.SemaphoreType.DMA(())],
        compiler_params=_params("parallel", "arbitrary"),
        name="ffn",
    )(*args)
    return outs if with_next else (outs[0], None)


def kernel(x_prompt, x_sample, cache_a_k, cache_a_v, cache_b_k, cache_b_v, c, c_ctx,
           mod_w, mod_b, norm_mix_pre, norm_mix_post, norm_ffn_pre, norm_ffn_post,
           w_in, w_out, rpb_a, sink_b, gmlp_ln_g, gmlp_ln_b, gmlp_w, gmlp_b,
           w_gate, w_up, w_down):
    cond = jnp.zeros((MOD_ROWS, D_MODEL), F32)
    cond = cond.at[:DEC_BATCH].set(c).at[CTX_MOD_ROW].set(c_ctx)
    mod5 = _modulation(cond, mod_w, mod_b).reshape(DEPTH, MOD_ROWS, N_MOD, 1, D_MODEL)

    vec = lambda a: a.reshape(DEPTH, 1, -1)
    g_mix_pre, g_mix_post = vec(norm_mix_pre), vec(norm_mix_post)
    g_ffn_pre, g_ffn_post = vec(norm_ffn_pre), vec(norm_ffn_post)
    ln_g, ln_b = vec(gmlp_ln_g), vec(gmlp_ln_b)
    gmlp_b4 = gmlp_b.reshape(DEPTH, G_C, CHUNK, 1)
    sink = sink_b.reshape(DEPTH * H_B)
    w_in_b, w_out_b = w_in.astype(BF16), w_out.astype(BF16)
    w_gate_b, w_up_b, w_down_b = w_gate.astype(BF16), w_up.astype(BF16), w_down.astype(BF16)
    bias = _na_bias(rpb_a)
    cos, sin = _rope_tables()

    tm_wide, tm_row = 1024, 512
    ctx_row = lambda i: CTX_MOD_ROW
    lat_row = lambda tm: (lambda i: (i * tm) // DEC_SEQ)

    xp = x_prompt.reshape(N_CTX_TOK, D_MODEL)
    xs = x_sample.reshape(N_LAT_TOK, D_MODEL)
    hp = _norm_mod(xp, mod5, g_mix_pre, 0, ctx_row, tm_row)
    hs = _norm_mod(xs, mod5, g_mix_pre, 0, lat_row(tm_row), tm_row)
    kv_all = [jnp.zeros((BATCH, DEPTH, nh, SEQ, HEAD_DIM), F32) for nh in (H_A, H_A, KV_B, KV_B)]
    for l in range(DEPTH):
        o, *kv_all = _ctx_mix(hp, w_in_b, sink, gmlp_w, gmlp_b4, ln_g, ln_b, kv_all, l)
        xp, hp = _out_proj([o], xp, mod5, g_mix_post, g_ffn_pre, w_out_b, l, ctx_row, tm_row)
        xp, hp = _ffn(hp, xp, mod5, g_ffn_post, g_mix_pre, w_gate_b, w_up_b, w_down_b, l,
                      ctx_row, tm_wide)
        z = _in_proj(hs, w_in_b, l, tm_wide)
        o_a = _lat_na(z, cache_a_k, cache_a_v, bias, l)
        o_b = _lat_win(z, cache_b_k, cache_b_v, sink, cos, sin, l)
        o_c = _lat_gmlp(z, gmlp_w, gmlp_b4, ln_g, ln_b, l)
        xs, hs = _out_proj([o_a, o_b, o_c], xs, mod5, g_mix_post, g_ffn_pre, w_out_b, l,
                           lat_row(tm_row), tm_row)
        xs, hs = _ffn(hs, xs, mod5, g_ffn_post, g_mix_pre, w_gate_b, w_up_b, w_down_b, l,
                      lat_row(tm_wide), tm_wide)

    return (xp.reshape(BATCH, SEQ, D_MODEL), xs.reshape(DEC_BATCH, DEC_SEQ, D_MODEL), *kv_all)
```

```python
import functools

import jax
import jax.numpy as jnp
import numpy as np
from jax import lax
from jax.experimental import pallas as pl
from jax.experimental.pallas import tpu as pltpu

D_MODEL = 2048
BATCH = 32
SEQ = 256
DEPTH = 4
DEC_BATCH = 8
DEC_SEQ = 1024
PAST_LEN = 256
GRID_W = 64
GRID_ROWS = DEC_SEQ // GRID_W
HEAD_DIM = 128
H_A = 6
H_B = 6
KV_B = 2
REP_B = H_B // KV_B
G_C = 4
C_GROUP = 128
W_A = H_A * HEAD_DIM
W_B = H_B * HEAD_DIM
W_C = G_C * C_GROUP
MIX_WIDTH = W_A + W_B + W_C
NA_ROWS = 8
NA_COLS = 16
WIN = 128
BLK = 128
CHUNK = 128
D_FF = 5632
N_MOD = 6
ROPE_BASE = 10000.0
NORM_EPS = 1e-6
ATTN_SCALE = HEAD_DIM ** -0.5
Q_A = 0
K_A = Q_A + W_A
V_A = K_A + W_A
Q_B = V_A + W_A
K_B = Q_B + W_B
V_B = K_B + KV_B * HEAD_DIM
U_C = V_B + KV_B * HEAD_DIM
V_C = U_C + W_C
IN_WIDTH = V_C + W_C

N_CTX_TOK = BATCH * SEQ
N_LAT_TOK = DEC_BATCH * DEC_SEQ
MOD_ROWS = 16
CTX_MOD_ROW = DEC_BATCH

VMEM_LIMIT_BYTES = 58 * 1024 * 1024
MXU_TILE = 256

BF16 = jnp.bfloat16
F32 = jnp.float32


def _params(*sem):
    return pltpu.CompilerParams(dimension_semantics=sem, vmem_limit_bytes=VMEM_LIMIT_BYTES)


def _dot(a, b):
    return jnp.dot(a, b, preferred_element_type=F32)


def _dot_nt(a, b):
    return lax.dot_general(a, b, (((1,), (1,)), ((), ())), preferred_element_type=F32)


def _rms(x, g):
    return (x * lax.rsqrt(jnp.mean(x * x, axis=-1, keepdims=True) + NORM_EPS)) * g


def _mod_kernel(c_ref, w_ref, b_ref, o_ref):
    s = jax.nn.silu(c_ref[...])
    o_ref[...] = _dot(s.astype(BF16), w_ref[...].astype(BF16)) + b_ref[...]


def _modulation(cond, mod_w, mod_b):
    tn = 1024
    n = N_MOD * D_MODEL
    return pl.pallas_call(
        _mod_kernel,
        grid=(DEPTH, n // tn),
        in_specs=[
            pl.BlockSpec((MOD_ROWS, D_MODEL), lambda l, j: (0, 0)),
            pl.BlockSpec((None, D_MODEL, tn), lambda l, j: (l, 0, j)),
            pl.BlockSpec((None, 1, tn), lambda l, j: (l, 0, j)),
        ],
        out_specs=pl.BlockSpec((None, MOD_ROWS, tn), lambda l, j: (l, 0, j)),
        out_shape=jax.ShapeDtypeStruct((DEPTH, MOD_ROWS, n), F32),
        compiler_params=_params("parallel", "parallel"),
        name="modulation",
    )(cond, mod_w, mod_b.reshape(DEPTH, 1, n))


def _mod_spec(l, which, row_fn):
    return pl.BlockSpec((None, None, None, 1, D_MODEL),
                        lambda i, *_: (l, row_fn(i), which, 0, 0))


def _vec_spec(l):
    return pl.BlockSpec((None, 1, D_MODEL), lambda i, *_: (l, 0, 0))


def _modulated_norm(x, g, scale, shift):
    return (_rms(x, g) * (1.0 + scale) + shift).astype(BF16)


def _norm_kernel(x_ref, g_ref, sc_ref, sh_ref, h_ref):
    h_ref[...] = _modulated_norm(x_ref[...], g_ref[...], sc_ref[...], sh_ref[...])


def _norm_mod(x, mod5, g_pre, l, row_fn, tm):
    m = x.shape[0]
    return pl.pallas_call(
        _norm_kernel,
        grid=(m // tm,),
        in_specs=[pl.BlockSpec((tm, D_MODEL), lambda i: (i, 0)),
                  _vec_spec(l), _mod_spec(l, 1, row_fn), _mod_spec(l, 0, row_fn)],
        out_specs=pl.BlockSpec((tm, D_MODEL), lambda i: (i, 0)),
        out_shape=jax.ShapeDtypeStruct((m, D_MODEL), BF16),
        compiler_params=_params("parallel"),
        name="norm_mod",
    )(x, g_pre, mod5, mod5)


def _in_kernel(h_ref, w_ref, z_ref):
    z_ref[...] = _dot(h_ref[...], w_ref[...])


def _in_proj(h, w_in, l, tm):
    m = h.shape[0]
    tn = 6 * MXU_TILE
    return pl.pallas_call(
        _in_kernel,
        grid=(IN_WIDTH // tn, m // tm),
        in_specs=[
            pl.BlockSpec((tm, D_MODEL), lambda j, i: (i, 0)),
            pl.BlockSpec((None, D_MODEL, tn), lambda j, i: (l, 0, j)),
        ],
        out_specs=pl.BlockSpec((tm, tn), lambda j, i: (i, j)),
        out_shape=jax.ShapeDtypeStruct((m, IN_WIDTH), F32),
        compiler_params=_params("parallel", "parallel"),
        name="in_proj",
    )(h, w_in)


def _softmax_pv(scores, values, extra_logit=None):
    m = scores[0].max(axis=-1, keepdims=True)
    for s in scores[1:]:
        m = jnp.maximum(m, s.max(axis=-1, keepdims=True))
    if extra_logit is not None:
        m = jnp.maximum(m, extra_logit)
    den = None
    acc = None
    for s, v in zip(scores, values):
        e = jnp.exp(s - m)
        d = e.sum(axis=-1, keepdims=True)
        a = _dot(e.astype(BF16), v)
        den = d if den is None else den + d
        acc = a if acc is None else acc + a
    if extra_logit is not None:
        den = den + jnp.exp(extra_logit - m)
    return acc / den


def _sink_column(sink_ref, base, rows_per_head):
    rows = lax.broadcasted_iota(jnp.int32, (REP_B * rows_per_head, 1), 0)
    col = jnp.full((REP_B * rows_per_head, 1), sink_ref[base], F32)
    for r in range(1, REP_B):
        col = jnp.where(rows >= r * rows_per_head, sink_ref[base + r], col)
    return col


def _gmlp(u_raw, v_raw, gw_ref, gb_ref, lng_ref, lnb_ref, write):
    u = jax.nn.gelu(u_raw)
    v = jax.nn.gelu(v_raw)
    mu = jnp.mean(v, axis=-1, keepdims=True)
    xc = v - mu
    vn = xc * lax.rsqrt(jnp.mean(xc * xc, axis=-1, keepdims=True) + NORM_EPS)
    vn = (vn * lng_ref[...] + lnb_ref[...]).astype(BF16)
    for n in range(u_raw.shape[0] // CHUNK):
        rs = slice(n * CHUNK, (n + 1) * CHUNK)
        for g in range(G_C):
            cs = slice(g * C_GROUP, (g + 1) * C_GROUP)
            s = _dot(gw_ref[g].astype(BF16), vn[rs, cs]) + gb_ref[g]
            write(n, g, u[rs, cs] * s)


CTX_PROJ_COLS = 2 * MXU_TILE
CTX_SEQS = 2


def _ctx_mix_kernel(l, sink_ref, h_ref, w_ref, gw_ref, gb_ref, lng_ref, lnb_ref,
                    ka_all, va_all, kb_all, vb_all, o_ref, ka_ref, va_ref, kb_ref, vb_ref, z_ref):
    del ka_all, va_all, kb_all, vb_all
    for s in range(CTX_SEQS):
        _ctx_mix_sequence(l, s, sink_ref, h_ref, w_ref, gw_ref, gb_ref, lng_ref, lnb_ref,
                          o_ref, ka_ref, va_ref, kb_ref, vb_ref, z_ref)


def _ctx_mix_sequence(l, s, sink_ref, h_ref, w_ref, gw_ref, gb_ref, lng_ref, lnb_ref,
                      o_ref, ka_ref, va_ref, kb_ref, vb_ref, z_ref):
    tok = slice(s * SEQ, (s + 1) * SEQ)
    h_in = h_ref[tok, :]
    for c0 in range(0, IN_WIDTH, CTX_PROJ_COLS):
        z_ref[tok, c0:c0 + CTX_PROJ_COLS] = _dot(h_in, w_ref[:, c0:c0 + CTX_PROJ_COLS])

    def col(off, i=0):
        return z_ref[tok, off + i * HEAD_DIM: off + (i + 1) * HEAD_DIM]

    for h in range(H_A):
        k = col(K_A, h)
        v = col(V_A, h)
        ka_ref[s, h] = k
        va_ref[s, h] = v
        sc = _dot_nt(col(Q_A, h).astype(BF16), k.astype(BF16)) * ATTN_SCALE
        o_ref[tok, h * HEAD_DIM:(h + 1) * HEAD_DIM] = _softmax_pv([sc], [v.astype(BF16)]).astype(BF16)

    for g in range(KV_B):
        k = col(K_B, g)
        v = col(V_B, g)
        kb_ref[s, g] = k
        vb_ref[s, g] = v
        q = jnp.concatenate([col(Q_B, g * REP_B + r) for r in range(REP_B)], axis=0).astype(BF16)
        sc = _dot_nt(q, k.astype(BF16)) * ATTN_SCALE
        sink = _sink_column(sink_ref, l * H_B + g * REP_B, SEQ)
        o = _softmax_pv([sc], [v.astype(BF16)], sink).astype(BF16)
        for r in range(REP_B):
            c0 = W_A + (g * REP_B + r) * HEAD_DIM
            o_ref[s * SEQ:(s + 1) * SEQ, c0:c0 + HEAD_DIM] = o[r * SEQ:(r + 1) * SEQ]

    def write(n, g, val):
        c0 = W_A + W_B + g * C_GROUP
        r0 = s * SEQ + n * CHUNK
        o_ref[r0:r0 + CHUNK, c0:c0 + C_GROUP] = val.astype(BF16)

    _gmlp(z_ref[tok, U_C:V_C], z_ref[tok, V_C:IN_WIDTH], gw_ref, gb_ref, lng_ref, lnb_ref, write)


def _gmlp_specs(l):
    return [
        pl.BlockSpec((None, G_C, CHUNK, CHUNK), lambda i, *_: (l, 0, 0, 0)),
        pl.BlockSpec((None, G_C, CHUNK, 1), lambda i, *_: (l, 0, 0, 0)),
        pl.BlockSpec((None, 1, W_C), lambda i, *_: (l, 0, 0)),
        pl.BlockSpec((None, 1, W_C), lambda i, *_: (l, 0, 0)),
    ]


def _ctx_mix(h, w_in, sink, gmlp_w, gmlp_b4, ln_g, ln_b, kv_all, l):
    kv_spec = lambda nh: pl.BlockSpec((CTX_SEQS, None, nh, SEQ, HEAD_DIM), lambda b: (b, l, 0, 0, 0))
    n_in = 7
    rows = CTX_SEQS * SEQ
    return pl.pallas_call(
        functools.partial(_ctx_mix_kernel, l),
        grid=(BATCH // CTX_SEQS,),
        in_specs=[
            pl.BlockSpec(memory_space=pltpu.SMEM),
            pl.BlockSpec((rows, D_MODEL), lambda b: (b, 0)),
            pl.BlockSpec((None, D_MODEL, IN_WIDTH), lambda b: (l, 0, 0), pipeline_mode=pl.Buffered(1)),
        ] + _gmlp_specs(l) + [pl.BlockSpec(memory_space=pl.ANY)] * 4,
        out_specs=[pl.BlockSpec((rows, MIX_WIDTH), lambda b: (b, 0)),
                   kv_spec(H_A), kv_spec(H_A), kv_spec(KV_B), kv_spec(KV_B)],
        out_shape=[jax.ShapeDtypeStruct((N_CTX_TOK, MIX_WIDTH), BF16)]
                  + [jax.ShapeDtypeStruct(a.shape, a.dtype) for a in kv_all],
        scratch_shapes=[pltpu.VMEM((rows, IN_WIDTH), F32)],
        input_output_aliases={n_in + i: 1 + i for i in range(4)},
        compiler_params=_params("parallel"),
        name="ctx_mix",
    )(sink, h, w_in, gmlp_w, gmlp_b4, ln_g, ln_b, *kv_all)


NA_QROWS = 4
NA_WROWS = 12
NA_QBLOCKS = GRID_ROWS // NA_QROWS


def _na_band_start(r):
    return min(max(r - NA_ROWS // 2, 0), GRID_ROWS - NA_ROWS)


def _na_window_start(qb):
    return min(max(qb * NA_QROWS - NA_ROWS // 2, 0), GRID_ROWS - NA_WROWS)


def _na_kernel(q_ref, k_ref, v_ref, ck_ref, cv_ref, bias_ref, o_ref):
    ck = ck_ref[...].astype(BF16)
    cv = cv_ref[...].astype(BF16)
    def window(qb):
        w0 = _na_window_start(qb)
        return slice(w0 * GRID_W, (w0 + NA_WROWS) * GRID_W)

    def query_rows(qb):
        return slice(qb * NA_QROWS * GRID_W, (qb + 1) * NA_QROWS * GRID_W)

    scores = []
    for qb in range(NA_QBLOCKS):
        q = q_ref[query_rows(qb), :].astype(BF16)
        s_loc = _dot_nt(q, k_ref[window(qb), :].astype(BF16)) * ATTN_SCALE + bias_ref[qb]
        s_ctx = _dot_nt(q, ck) * ATTN_SCALE
        scores.append((s_loc, s_ctx))
    for qb in range(NA_QBLOCKS):
        o = _softmax_pv(list(scores[qb]), [v_ref[window(qb), :].astype(BF16), cv])
        o_ref[query_rows(qb), :] = o.astype(BF16)


def _na_bias(rpb):
    n_dc = 2 * NA_COLS - 1
    edge = GRID_W - NA_COLS
    ext = jnp.concatenate([jnp.repeat(rpb[..., :1], edge, axis=-1), rpb,
                           jnp.repeat(rpb[..., -1:], edge, axis=-1)], axis=-1).astype(F32)
    toep = jnp.stack([ext[..., GRID_W - 1 - q: 2 * GRID_W - 1 - q] for q in range(GRID_W)], axis=-2)
    col = np.arange(GRID_W)
    c0 = np.clip(col - NA_COLS // 2, 0, GRID_W - NA_COLS)
    col_ok = (col[None, :] >= c0[:, None]) & (col[None, :] < c0[:, None] + NA_COLS)
    toep = jnp.where(col_ok, toep, -jnp.inf)
    by_q = toep.transpose(0, 1, 3, 2, 4).reshape(DEPTH, H_A, GRID_W, (2 * NA_ROWS - 1) * GRID_W)
    per_row = []
    for r in range(GRID_ROWS):
        w0 = _na_window_start(r // NA_QROWS)
        r0 = _na_band_start(r)
        a0 = r0 - r + NA_ROWS - 1
        band = by_q[..., a0 * GRID_W:(a0 + NA_ROWS) * GRID_W]
        left = (r0 - w0) * GRID_W
        right = (NA_WROWS - NA_ROWS) * GRID_W - left
        per_row.append(jnp.pad(band, ((0, 0), (0, 0), (0, 0), (left, right)),
                               constant_values=-jnp.inf))
    bias = jnp.stack(per_row, axis=2)
    return bias.reshape(DEPTH, H_A, NA_QBLOCKS, NA_QROWS * GRID_W, NA_WROWS * GRID_W)


def _lat_na(z, cache_k, cache_v, bias, l):
    zc = lambda off: pl.BlockSpec((DEC_SEQ, HEAD_DIM), lambda h, b: (b, off // HEAD_DIM + h))
    cache = pl.BlockSpec((None, None, None, PAST_LEN, HEAD_DIM), lambda h, b: (b, l, h, 0, 0))
    return pl.pallas_call(
        _na_kernel,
        grid=(H_A, DEC_BATCH),
        in_specs=[zc(Q_A), zc(K_A), zc(V_A), cache, cache,
                  pl.BlockSpec((None, None, NA_QBLOCKS, NA_QROWS * GRID_W, NA_WROWS * GRID_W),
                               lambda h, b: (l, h, 0, 0, 0))],
        out_specs=pl.BlockSpec((DEC_SEQ, HEAD_DIM), lambda h, b: (b, h)),
        out_shape=jax.ShapeDtypeStruct((N_LAT_TOK, W_A), BF16),
        compiler_params=_params("parallel", "parallel"),
        name="lat_na",
    )(z, z, z, cache_k, cache_v, bias)


def _rope(x, cos, sin_signed):
    lane = lax.broadcasted_iota(jnp.int32, x.shape, 1)
    partner = jnp.where(lane % (HEAD_DIM // 2) < HEAD_DIM // 4,
                        pltpu.roll(x, HEAD_DIM - HEAD_DIM // 4, 1),
                        pltpu.roll(x, HEAD_DIM // 4, 1))
    return x * cos + partner * sin_signed


def _rope_tables():
    half = HEAD_DIM // 2
    quarter = half // 2
    t = np.arange(DEC_SEQ)
    freqs = ROPE_BASE ** (-jnp.arange(quarter, dtype=F32) / quarter)

    def tables(pos):
        ang = jnp.asarray(pos, F32)[:, None] * freqs[None, :]
        cos, sin = jnp.cos(ang), jnp.sin(ang)
        return jnp.concatenate([cos, cos], axis=-1), jnp.concatenate([-sin, sin], axis=-1)

    cr, sr = tables(t // GRID_W)
    cc, sc = tables(t % GRID_W)
    return jnp.concatenate([cr, cc], axis=-1), jnp.concatenate([sr, sc], axis=-1)


def _win_kernel(l, sink_ref, q_ref, k_ref, v_ref, ck_ref, cv_ref, cos_ref, sin_ref, o_ref, kr_ref):
    g = pl.program_id(1)
    ck = ck_ref[...].astype(BF16)
    cv = cv_ref[...].astype(BF16)
    kr_ref[...] = _rope(k_ref[...], cos_ref[...], sin_ref[...]).astype(BF16)
    sink = _sink_column(sink_ref, l * H_B + g * REP_B, BLK)
    n_blk = DEC_SEQ // BLK
    span = BLK + 2 * WIN

    def window_bias(first_key_offset):
        qpos = first_key_offset + lax.broadcasted_iota(jnp.int32, (REP_B * BLK, span), 0) % BLK
        kpos = lax.broadcasted_iota(jnp.int32, (REP_B * BLK, span), 1)
        return jnp.where(jnp.abs(qpos - kpos) <= WIN, 0.0, -jnp.inf).astype(F32)

    bias_first = window_bias(0)
    bias_rest = window_bias(BLK)
    def key_span(nb):
        return slice(max(nb - 1, 0) * BLK, min(nb + 2, n_blk) * BLK)

    def block_scores(nb):
        rows = slice(nb * BLK, (nb + 1) * BLK)
        span_nb = key_span(nb)
        cos = cos_ref[rows, :]
        sin = sin_ref[rows, :]
        q = jnp.concatenate(
            [_rope(q_ref[rows, r * HEAD_DIM:(r + 1) * HEAD_DIM], cos, sin) for r in range(REP_B)],
            axis=0).astype(BF16)
        bias = (bias_first if nb == 0 else bias_rest)[:, :span_nb.stop - span_nb.start]
        s_loc = _dot_nt(q, kr_ref[span_nb, :]) * ATTN_SCALE + bias
        s_ctx = _dot_nt(q, ck) * ATTN_SCALE
        return s_loc, s_ctx

    def block_output(nb, scores):
        rows = slice(nb * BLK, (nb + 1) * BLK)
        o = _softmax_pv(list(scores), [v_ref[key_span(nb), :].astype(BF16), cv], sink).astype(BF16)
        for r in range(REP_B):
            o_ref[rows, r * HEAD_DIM:(r + 1) * HEAD_DIM] = o[r * BLK:(r + 1) * BLK]

    pending = block_scores(0)
    for nb in range(n_blk):
        upcoming = block_scores(nb + 1) if nb + 1 < n_blk else None
        block_output(nb, pending)
        pending = upcoming


def _lat_win(z, cache_k, cache_v, sink, cos, sin, l):
    zc = lambda off: pl.BlockSpec((DEC_SEQ, HEAD_DIM), lambda b, g: (b, off // HEAD_DIM + g))
    cache = pl.BlockSpec((None, None, None, PAST_LEN, HEAD_DIM), lambda b, g: (b, l, g, 0, 0))
    tab = pl.BlockSpec((DEC_SEQ, HEAD_DIM), lambda b, g: (0, 0))
    qw = REP_B * HEAD_DIM
    return pl.pallas_call(
        functools.partial(_win_kernel, l),
        grid=(DEC_BATCH, KV_B),
        in_specs=[pl.BlockSpec(memory_space=pltpu.SMEM),
                  pl.BlockSpec((DEC_SEQ, qw), lambda b, g: (b, Q_B // qw + g)),
                  zc(K_B), zc(V_B), cache, cache, tab, tab],
        out_specs=pl.BlockSpec((DEC_SEQ, qw), lambda b, g: (b, g)),
        out_shape=jax.ShapeDtypeStruct((N_LAT_TOK, W_B), BF16),
        scratch_shapes=[pltpu.VMEM((DEC_SEQ, HEAD_DIM), BF16)],
        compiler_params=_params("parallel", "parallel"),
        name="lat_win",
    )(sink, z, z, z, cache_k, cache_v, cos, sin)


def _gmlp_kernel(u_ref, v_ref, gw_ref, gb_ref, lng_ref, lnb_ref, o_ref):
    def write(n, g, val):
        o_ref[n * CHUNK:(n + 1) * CHUNK, g * C_GROUP:(g + 1) * C_GROUP] = val.astype(BF16)

    _gmlp(u_ref[...], v_ref[...], gw_ref, gb_ref, lng_ref, lnb_ref, write)


def _lat_gmlp(z, gmlp_w, gmlp_b4, ln_g, ln_b, l):
    rows = DEC_SEQ
    return pl.pallas_call(
        _gmlp_kernel,
        grid=(N_LAT_TOK // rows,),
        in_specs=[pl.BlockSpec((rows, W_C), lambda i: (i, U_C // W_C)),
                  pl.BlockSpec((rows, W_C), lambda i: (i, V_C // W_C))] + _gmlp_specs(l),
        out_specs=pl.BlockSpec((rows, W_C), lambda i: (i, 0)),
        out_shape=jax.ShapeDtypeStruct((N_LAT_TOK, W_C), BF16),
        compiler_params=_params("parallel"),
        name="lat_gmlp",
    )(z, z, gmlp_w, gmlp_b4, ln_g, ln_b)


EPI_ROWS = 128


def _residual_epilogue(y, rows, x_ref, g_post_ref, gate_ref, xo_ref, next_norm):
    x_new = x_ref[rows, :] + _rms(y, gate_ref[...] * g_post_ref[...])
    xo_ref[rows, :] = x_new
    if next_norm is not None:
        g_ref, sc_ref, sh_ref, h_ref = next_norm
        h_ref[rows, :] = (_rms(x_new, g_ref[...] * (1.0 + sc_ref[...])) + sh_ref[...]).astype(BF16)


def _out_kernel(widths, *refs):
    o_refs = refs[:len(widths)]
    (w_ref, x_ref, g_post_ref, gate_ref, g_next_ref, sc_ref, sh_ref,
     xo_ref, h_ref) = refs[len(widths):]
    for r in range(x_ref.shape[0] // EPI_ROWS):
        rows = slice(r * EPI_ROWS, (r + 1) * EPI_ROWS)
        y = None
        off = 0
        for o_ref, wd in zip(o_refs, widths):
            part = _dot(o_ref[rows, :], w_ref[off:off + wd, :])
            y = part if y is None else y + part
            off += wd
        _residual_epilogue(y, rows, x_ref, g_post_ref, gate_ref, xo_ref,
                           (g_next_ref, sc_ref, sh_ref, h_ref))


def _out_proj(o_list, x, mod5, g_post, g_ffn_pre, w_out, l, row_fn, tm):
    m = x.shape[0]
    widths = tuple(o.shape[1] for o in o_list)
    row_spec = pl.BlockSpec((tm, D_MODEL), lambda i: (i, 0))
    return pl.pallas_call(
        functools.partial(_out_kernel, widths),
        grid=(m // tm,),
        in_specs=[pl.BlockSpec((tm, wd), lambda i: (i, 0)) for wd in widths] + [
            pl.BlockSpec((None, MIX_WIDTH, D_MODEL), lambda i: (l, 0, 0)),
            row_spec,
            _vec_spec(l),
            _mod_spec(l, 2, row_fn),
            _vec_spec(l),
            _mod_spec(l, 4, row_fn),
            _mod_spec(l, 3, row_fn),
        ],
        out_specs=[row_spec, row_spec],
        out_shape=[jax.ShapeDtypeStruct((m, D_MODEL), F32),
                   jax.ShapeDtypeStruct((m, D_MODEL), BF16)],
        compiler_params=_params("parallel"),
        name="out_proj",
    )(*o_list, w_out, x, g_post, mod5, g_ffn_pre, mod5, mod5)


FFN_TC = 2 * MXU_TILE
FFN_ROWS = 512
FFN_LAST_ROWS = 256


def _ffn_kernel(with_next, h_ref, wg_ref, wu_ref, wd_ref, x_hbm, g_post_ref, gate_ref, *refs):
    if with_next:
        g_next_ref, sc_ref, sh_ref, xo_ref, ho_ref, x_buf, x_sem = refs
        next_norm = (g_next_ref, sc_ref, sh_ref, ho_ref)
    else:
        xo_ref, x_buf, x_sem = refs
        next_norm = None
    tm = x_buf.shape[0]
    c = pl.program_id(1)
    last = pl.num_programs(1) - 1

    def x_copy():
        row0 = pl.multiple_of(pl.program_id(0) * tm, tm)
        return pltpu.make_async_copy(x_hbm.at[pl.ds(row0, tm), :], x_buf, x_sem)

    def chunk_product(rows):
        h = h_ref[rows, :]
        a = (jax.nn.silu(_dot(h, wg_ref[...])) * _dot(h, wu_ref[...])).astype(BF16)
        return _dot(a, wd_ref[...])

    def row_blocks(n):
        for r in range(tm // n):
            yield slice(r * n, (r + 1) * n)

    @pl.when(c == 0)
    def _():
        x_copy().start()
        for rows in row_blocks(FFN_ROWS):
            xo_ref[rows, :] = chunk_product(rows)

    @pl.when((c > 0) & (c < last))
    def _():
        for rows in row_blocks(FFN_ROWS):
            xo_ref[rows, :] += chunk_product(rows)

    @pl.when(c == last)
    def _():
        x_copy().wait()
        for rows in row_blocks(FFN_LAST_ROWS):
            y = xo_ref[rows, :] + chunk_product(rows)
            _residual_epilogue(y, rows, x_buf, g_post_ref, gate_ref, xo_ref, next_norm)


def _ffn(h, x, mod5, g_post, g_mix_pre, w_gate, w_up, w_down, l, row_fn, tm):
    m = x.shape[0]
    with_next = l + 1 < DEPTH
    row_spec = pl.BlockSpec((tm, D_MODEL), lambda i, c: (i, 0))
    up_spec = pl.BlockSpec((None, D_MODEL, FFN_TC), lambda i, c: (l, 0, c))
    in_specs = [
        row_spec, up_spec, up_spec,
        pl.BlockSpec((None, FFN_TC, D_MODEL), lambda i, c: (l, c, 0)),
        pl.BlockSpec(memory_space=pl.ANY),
        _vec_spec(l),
        _mod_spec(l, 5, row_fn),
    ]
    args = [h, w_gate, w_up, w_down, x, g_post, mod5]
    out_specs = [row_spec]
    out_shape = [jax.ShapeDtypeStruct((m, D_MODEL), F32)]
    if with_next:
        in_specs += [_vec_spec(l + 1), _mod_spec(l + 1, 1, row_fn), _mod_spec(l + 1, 0, row_fn)]
        args += [g_mix_pre, mod5, mod5]
        out_specs.append(row_spec)
        out_shape.append(jax.ShapeDtypeStruct((m, D_MODEL), BF16))
    outs = pl.pallas_call(
        functools.partial(_ffn_kernel, with_next),
        grid=(m // tm, D_FF // FFN_TC),
        in_specs=in_specs,
        out_specs=out_specs,
        out_shape=out_shape,
        scratch_shapes=[pltpu.VMEM((tm, D_MODEL), F32), pltpu.SemaphoreType.DMA(())],
        compiler_params=_params("parallel", "arbitrary"),
        name="ffn",
    )(*args)
    return outs if with_next else (outs[0], None)


def kernel(x_prompt, x_sample, cache_a_k, cache_a_v, cache_b_k, cache_b_v, c, c_ctx,
           mod_w, mod_b, norm_mix_pre, norm_mix_post, norm_ffn_pre, norm_ffn_post,
           w_in, w_out, rpb_a, sink_b, gmlp_ln_g, gmlp_ln_b, gmlp_w, gmlp_b,
           w_gate, w_up, w_down):
    cond = jnp.zeros((MOD_ROWS, D_MODEL), F32)
    cond = cond.at[:DEC_BATCH].set(c).at[CTX_MOD_ROW].set(c_ctx)
    mod5 = _modulation(cond, mod_w, mod_b).reshape(DEPTH, MOD_ROWS, N_MOD, 1, D_MODEL)

    vec = lambda a: a.reshape(DEPTH, 1, -1)
    g_mix_pre, g_mix_post = vec(norm_mix_pre), vec(norm_mix_post)
    g_ffn_pre, g_ffn_post = vec(norm_ffn_pre), vec(norm_ffn_post)
    ln_g, ln_b = vec(gmlp_ln_g), vec(gmlp_ln_b)
    gmlp_b4 = gmlp_b.reshape(DEPTH, G_C, CHUNK, 1)
    sink = sink_b.reshape(DEPTH * H_B)
    w_in_b, w_out_b = w_in.astype(BF16), w_out.astype(BF16)
    w_gate_b, w_up_b, w_down_b = w_gate.astype(BF16), w_up.astype(BF16), w_down.astype(BF16)
    bias = _na_bias(rpb_a)
    cos, sin = _rope_tables()

    tm_wide, tm_row = 1024, 512
    ctx_row = lambda i: CTX_MOD_ROW
    lat_row = lambda tm: (lambda i: (i * tm) // DEC_SEQ)

    xp = x_prompt.reshape(N_CTX_TOK, D_MODEL)
    xs = x_sample.reshape(N_LAT_TOK, D_MODEL)
    hp = _norm_mod(xp, mod5, g_mix_pre, 0, ctx_row, tm_row)
    hs = _norm_mod(xs, mod5, g_mix_pre, 0, lat_row(tm_row), tm_row)
    kv_all = [jnp.zeros((BATCH, DEPTH, nh, SEQ, HEAD_DIM), F32) for nh in (H_A, H_A, KV_B, KV_B)]
    for l in range(DEPTH):
        o, *kv_all = _ctx_mix(hp, w_in_b, sink, gmlp_w, gmlp_b4, ln_g, ln_b, kv_all, l)
        xp, hp = _out_proj([o], xp, mod5, g_mix_post, g_ffn_pre, w_out_b, l, ctx_row, tm_row)
        xp, hp = _ffn(hp, xp, mod5, g_ffn_post, g_mix_pre, w_gate_b, w_up_b, w_down_b, l,
                      ctx_row, tm_wide)
        z = _in_proj(hs, w_in_b, l, tm_wide)
        o_a = _lat_na(z, cache_a_k, cache_a_v, bias, l)
        o_b = _lat_win(z, cache_b_k, cache_b_v, sink, cos, sin, l)
        o_c = _lat_gmlp(z, gmlp_w, gmlp_b4, ln_g, ln_b, l)
        xs, hs = _out_proj([o_a, o_b, o_c], xs, mod5, g_mix_post, g_ffn_pre, w_out_b, l,
                           lat_row(tm_row), tm_row)
        xs, hs = _ffn(hs, xs, mod5, g_ffn_post, g_mix_pre, w_gate_b, w_up_b, w_down_b, l,
                      lat_row(tm_wide), tm_wide)

    return (xp.reshape(BATCH, SEQ, D_MODEL), xs.reshape(DEC_BATCH, DEC_SEQ, D_MODEL), *kv_all)
```

```python
import functools

import jax
import jax.numpy as jnp
import numpy as np
from jax import lax
from jax.experimental import pallas as pl
from jax.experimental.pallas import tpu as pltpu

D_MODEL = 2048
BATCH = 32
SEQ = 256
DEPTH = 4
DEC_BATCH = 8
DEC_SEQ = 1024
PAST_LEN = 256
GRID_W = 64
GRID_ROWS = DEC_SEQ // GRID_W
HEAD_DIM = 128
H_A = 6
H_B = 6
KV_B = 2
REP_B = H_B // KV_B
G_C = 4
C_GROUP = 128
W_A = H_A * HEAD_DIM
W_B = H_B * HEAD_DIM
W_C = G_C * C_GROUP
MIX_WIDTH = W_A + W_B + W_C
NA_ROWS = 8
NA_COLS = 16
WIN = 128
BLK = 128
CHUNK = 128
D_FF = 5632
N_MOD = 6
ROPE_BASE = 10000.0
NORM_EPS = 1e-6
ATTN_SCALE = HEAD_DIM ** -0.5
Q_A = 0
K_A = Q_A + W_A
V_A = K_A + W_A
Q_B = V_A + W_A
K_B = Q_B + W_B
V_B = K_B + KV_B * HEAD_DIM
U_C = V_B + KV_B * HEAD_DIM
V_C = U_C + W_C
IN_WIDTH = V_C + W_C

N_CTX_TOK = BATCH * SEQ
N_LAT_TOK = DEC_BATCH * DEC_SEQ
MOD_ROWS = 16
CTX_MOD_ROW = DEC_BATCH

VMEM_LIMIT_BYTES = 58 * 1024 * 1024
MXU_TILE = 256

BF16 = jnp.bfloat16
F32 = jnp.float32


def _params(*sem):
    return pltpu.CompilerParams(dimension_semantics=sem, vmem_limit_bytes=VMEM_LIMIT_BYTES)


def _dot(a, b):
    return jnp.dot(a, b, preferred_element_type=F32)


def _dot_nt(a, b):
    return lax.dot_general(a, b, (((1,), (1,)), ((), ())), preferred_element_type=F32)


def _rms(x, g):
    return (x * lax.rsqrt(jnp.mean(x * x, axis=-1, keepdims=True) + NORM_EPS)) * g


def _mod_kernel(c_ref, w_ref, b_ref, o_ref):
    s = jax.nn.silu(c_ref[...])
    o_ref[...] = _dot(s.astype(BF16), w_ref[...].astype(BF16)) + b_ref[...]


def _modulation(cond, mod_w, mod_b):
    tn = 1024
    n = N_MOD * D_MODEL
    return pl.pallas_call(
        _mod_kernel,
        grid=(DEPTH, n // tn),
        in_specs=[
            pl.BlockSpec((MOD_ROWS, D_MODEL), lambda l, j: (0, 0)),
            pl.BlockSpec((None, D_MODEL, tn), lambda l, j: (l, 0, j)),
            pl.BlockSpec((None, 1, tn), lambda l, j: (l, 0, j)),
        ],
        out_specs=pl.BlockSpec((None, MOD_ROWS, tn), lambda l, j: (l, 0, j)),
        out_shape=jax.ShapeDtypeStruct((DEPTH, MOD_ROWS, n), F32),
        compiler_params=_params("parallel", "parallel"),
        name="modulation",
    )(cond, mod_w, mod_b.reshape(DEPTH, 1, n))


def _mod_spec(l, which, row_fn):
    return pl.BlockSpec((None, None, None, 1, D_MODEL),
                        lambda i, *_: (l, row_fn(i), which, 0, 0))


def _vec_spec(l):
    return pl.BlockSpec((None, 1, D_MODEL), lambda i, *_: (l, 0, 0))


def _modulated_norm(x, g, scale, shift):
    return (_rms(x, g) * (1.0 + scale) + shift).astype(BF16)


def _norm_kernel(x_ref, g_ref, sc_ref, sh_ref, h_ref):
    h_ref[...] = _modulated_norm(x_ref[...], g_ref[...], sc_ref[...], sh_ref[...])


def _norm_mod(x, mod5, g_pre, l, row_fn, tm):
    m = x.shape[0]
    return pl.pallas_call(
        _norm_kernel,
        grid=(m // tm,),
        in_specs=[pl.BlockSpec((tm, D_MODEL), lambda i: (i, 0)),
                  _vec_spec(l), _mod_spec(l, 1, row_fn), _mod_spec(l, 0, row_fn)],
        out_specs=pl.BlockSpec((tm, D_MODEL), lambda i: (i, 0)),
        out_shape=jax.ShapeDtypeStruct((m, D_MODEL), BF16),
        compiler_params=_params("parallel"),
        name="norm_mod",
    )(x, g_pre, mod5, mod5)


def _in_kernel(h_ref, w_ref, z_ref):
    z_ref[...] = _dot(h_ref[...], w_ref[...])


def _in_proj(h, w_in, l, tm):
    m = h.shape[0]
    tn = 6 * MXU_TILE
    return pl.pallas_call(
        _in_kernel,
        grid=(IN_WIDTH // tn, m // tm),
        in_specs=[
            pl.BlockSpec((tm, D_MODEL), lambda j, i: (i, 0)),
            pl.BlockSpec((None, D_MODEL, tn), lambda j, i: (l, 0, j)),
        ],
        out_specs=pl.BlockSpec((tm, tn), lambda j, i: (i, j)),
        out_shape=jax.ShapeDtypeStruct((m, IN_WIDTH), F32),
        compiler_params=_params("parallel", "parallel"),
        name="in_proj",
    )(h, w_in)


def _with_ones(v):
    return jnp.concatenate([v.astype(BF16), jnp.ones(v.shape, BF16)], axis=-1)


def _softmax_pv(scores, values, extra_logit=None):
    m = scores[0].max(axis=-1, keepdims=True)
    for s in scores[1:]:
        m = jnp.maximum(m, s.max(axis=-1, keepdims=True))
    if extra_logit is not None:
        m = jnp.maximum(m, extra_logit)
    acc = None
    for s, v in zip(scores, values):
        a = _dot(jnp.exp(s - m).astype(BF16), v)
        acc = a if acc is None else acc + a
    num, den = acc[:, :HEAD_DIM], acc[:, HEAD_DIM:]
    if extra_logit is not None:
        den = den + jnp.exp(extra_logit - m)
    return num / den


def _sink_column(sink_ref, base, rows_per_head):
    rows = lax.broadcasted_iota(jnp.int32, (REP_B * rows_per_head, 1), 0)
    col = jnp.full((REP_B * rows_per_head, 1), sink_ref[base], F32)
    for r in range(1, REP_B):
        col = jnp.where(rows >= r * rows_per_head, sink_ref[base + r], col)
    return col


def _gmlp(u_raw, v_raw, gw_ref, gb_ref, lng_ref, lnb_ref, write):
    u = jax.nn.gelu(u_raw)
    v = jax.nn.gelu(v_raw)
    mu = jnp.mean(v, axis=-1, keepdims=True)
    xc = v - mu
    vn = xc * lax.rsqrt(jnp.mean(xc * xc, axis=-1, keepdims=True) + NORM_EPS)
    vn = (vn * lng_ref[...] + lnb_ref[...]).astype(BF16)
    for n in range(u_raw.shape[0] // CHUNK):
        rs = slice(n * CHUNK, (n + 1) * CHUNK)
        for g in range(G_C):
            cs = slice(g * C_GROUP, (g + 1) * C_GROUP)
            s = _dot(gw_ref[g].astype(BF16), vn[rs, cs]) + gb_ref[g]
            write(n, g, u[rs, cs] * s)


CTX_PROJ_COLS = 2 * MXU_TILE
def _ctx_seqs(l):
    return 1 if l == 0 else 2


def _ctx_mix_kernel(l, sink_ref, h_ref, w_ref, gw_ref, gb_ref, lng_ref, lnb_ref, *refs):
    o_ref, ka_ref, va_ref, kb_ref, vb_ref, z_ref = refs[-6:]
    kv_refs = (ka_ref, va_ref, kb_ref, vb_ref)
    if l == 0:
        for ref in kv_refs:
            ref[:, 1:] = jnp.zeros((ref.shape[0], DEPTH - 1) + ref.shape[2:], F32)
        kv_refs = tuple(ref.at[:, 0] for ref in kv_refs)
    for s in range(_ctx_seqs(l)):
        _ctx_mix_sequence(l, s, sink_ref, h_ref, w_ref, gw_ref, gb_ref, lng_ref, lnb_ref,
                          o_ref, *kv_refs, z_ref)


def _ctx_mix_sequence(l, s, sink_ref, h_ref, w_ref, gw_ref, gb_ref, lng_ref, lnb_ref,
                      o_ref, ka_ref, va_ref, kb_ref, vb_ref, z_ref):
    tok = slice(s * SEQ, (s + 1) * SEQ)
    h_in = h_ref[tok, :]
    for c0 in range(0, IN_WIDTH, CTX_PROJ_COLS):
        z_ref[tok, c0:c0 + CTX_PROJ_COLS] = _dot(h_in, w_ref[:, c0:c0 + CTX_PROJ_COLS])

    def col(off, i=0):
        return z_ref[tok, off + i * HEAD_DIM: off + (i + 1) * HEAD_DIM]

    for h in range(H_A):
        k = col(K_A, h)
        v = col(V_A, h)
        ka_ref[s, h] = k
        va_ref[s, h] = v
        sc = _dot_nt(col(Q_A, h).astype(BF16), k.astype(BF16)) * ATTN_SCALE
        o_ref[tok, h * HEAD_DIM:(h + 1) * HEAD_DIM] = _softmax_pv([sc], [_with_ones(v)]).astype(BF16)

    for g in range(KV_B):
        k = col(K_B, g)
        v = col(V_B, g)
        kb_ref[s, g] = k
        vb_ref[s, g] = v
        q = jnp.concatenate([col(Q_B, g * REP_B + r) for r in range(REP_B)], axis=0).astype(BF16)
        sc = _dot_nt(q, k.astype(BF16)) * ATTN_SCALE
        sink = _sink_column(sink_ref, l * H_B + g * REP_B, SEQ)
        o = _softmax_pv([sc], [_with_ones(v)], sink).astype(BF16)
        for r in range(REP_B):
            c0 = W_A + (g * REP_B + r) * HEAD_DIM
            o_ref[s * SEQ:(s + 1) * SEQ, c0:c0 + HEAD_DIM] = o[r * SEQ:(r + 1) * SEQ]

    def write(n, g, val):
        c0 = W_A + W_B + g * C_GROUP
        r0 = s * SEQ + n * CHUNK
        o_ref[r0:r0 + CHUNK, c0:c0 + C_GROUP] = val.astype(BF16)

    _gmlp(z_ref[tok, U_C:V_C], z_ref[tok, V_C:IN_WIDTH], gw_ref, gb_ref, lng_ref, lnb_ref, write)


def _gmlp_specs(l):
    return [
        pl.BlockSpec((None, G_C, CHUNK, CHUNK), lambda i, *_: (l, 0, 0, 0)),
        pl.BlockSpec((None, G_C, CHUNK, 1), lambda i, *_: (l, 0, 0, 0)),
        pl.BlockSpec((None, 1, W_C), lambda i, *_: (l, 0, 0)),
        pl.BlockSpec((None, 1, W_C), lambda i, *_: (l, 0, 0)),
    ]


def _ctx_mix(h, w_in, sink, gmlp_w, gmlp_b4, ln_g, ln_b, kv_all, l):
    seqs = _ctx_seqs(l)
    rows = seqs * SEQ
    kv_heads = (H_A, H_A, KV_B, KV_B)
    if l == 0:
        kv_spec = lambda nh: pl.BlockSpec((seqs, DEPTH, nh, SEQ, HEAD_DIM), lambda b: (b, 0, 0, 0, 0))
        kv_in, aliases = [], {}
    else:
        kv_spec = lambda nh: pl.BlockSpec((seqs, None, nh, SEQ, HEAD_DIM), lambda b: (b, l, 0, 0, 0))
        kv_in = list(kv_all)
        n_in = 7
        aliases = {n_in + i: 1 + i for i in range(4)}
    return pl.pallas_call(
        functools.partial(_ctx_mix_kernel, l),
        grid=(BATCH // seqs,),
        in_specs=[
            pl.BlockSpec(memory_space=pltpu.SMEM),
            pl.BlockSpec((rows, D_MODEL), lambda b: (b, 0)),
            pl.BlockSpec((None, D_MODEL, IN_WIDTH), lambda b: (l, 0, 0), pipeline_mode=pl.Buffered(1)),
        ] + _gmlp_specs(l) + [pl.BlockSpec(memory_space=pl.ANY)] * len(kv_in),
        out_specs=[pl.BlockSpec((rows, MIX_WIDTH), lambda b: (b, 0))] + [kv_spec(nh) for nh in kv_heads],
        out_shape=[jax.ShapeDtypeStruct((N_CTX_TOK, MIX_WIDTH), BF16)]
                  + [jax.ShapeDtypeStruct((BATCH, DEPTH, nh, SEQ, HEAD_DIM), F32) for nh in kv_heads],
        scratch_shapes=[pltpu.VMEM((rows, IN_WIDTH), F32)],
        input_output_aliases=aliases,
        compiler_params=_params("parallel"),
        name="ctx_mix",
    )(sink, h, w_in, gmlp_w, gmlp_b4, ln_g, ln_b, *kv_in)


NA_QROWS = 4
NA_WROWS = 12
NA_QBLOCKS = GRID_ROWS // NA_QROWS


def _na_band_start(r):
    return min(max(r - NA_ROWS // 2, 0), GRID_ROWS - NA_ROWS)


def _na_window_start(qb):
    return min(max(qb * NA_QROWS - NA_ROWS // 2, 0), GRID_ROWS - NA_WROWS)


def _na_kernel(q_ref, k_ref, v_ref, ck_ref, cv_ref, bias_ref, o_ref):
    ck = ck_ref[...].astype(BF16)
    cv = _with_ones(cv_ref[...])
    v_aug = _with_ones(v_ref[...])

    def window(qb):
        w0 = _na_window_start(qb)
        return slice(w0 * GRID_W, (w0 + NA_WROWS) * GRID_W)

    def query_rows(qb):
        return slice(qb * NA_QROWS * GRID_W, (qb + 1) * NA_QROWS * GRID_W)

    scores = []
    for qb in range(NA_QBLOCKS):
        q = q_ref[query_rows(qb), :].astype(BF16)
        s_loc = _dot_nt(q, k_ref[window(qb), :].astype(BF16)) * ATTN_SCALE + bias_ref[qb]
        s_ctx = _dot_nt(q, ck) * ATTN_SCALE
        scores.append((s_loc, s_ctx))
    for qb in range(NA_QBLOCKS):
        o = _softmax_pv(list(scores[qb]), [v_aug[window(qb), :], cv])
        o_ref[query_rows(qb), :] = o.astype(BF16)


def _na_bias(rpb):
    n_dc = 2 * NA_COLS - 1
    edge = GRID_W - NA_COLS
    ext = jnp.concatenate([jnp.repeat(rpb[..., :1], edge, axis=-1), rpb,
                           jnp.repeat(rpb[..., -1:], edge, axis=-1)], axis=-1).astype(F32)
    toep = jnp.stack([ext[..., GRID_W - 1 - q: 2 * GRID_W - 1 - q] for q in range(GRID_W)], axis=-2)
    col = np.arange(GRID_W)
    c0 = np.clip(col - NA_COLS // 2, 0, GRID_W - NA_COLS)
    col_ok = (col[None, :] >= c0[:, None]) & (col[None, :] < c0[:, None] + NA_COLS)
    toep = jnp.where(col_ok, toep, -jnp.inf)
    by_q = toep.transpose(0, 1, 3, 2, 4).reshape(DEPTH, H_A, GRID_W, (2 * NA_ROWS - 1) * GRID_W)
    per_row = []
    for r in range(GRID_ROWS):
        w0 = _na_window_start(r // NA_QROWS)
        r0 = _na_band_start(r)
        a0 = r0 - r + NA_ROWS - 1
        band = by_q[..., a0 * GRID_W:(a0 + NA_ROWS) * GRID_W]
        left = (r0 - w0) * GRID_W
        right = (NA_WROWS - NA_ROWS) * GRID_W - left
        per_row.append(jnp.pad(band, ((0, 0), (0, 0), (0, 0), (left, right)),
                               constant_values=-jnp.inf))
    bias = jnp.stack(per_row, axis=2)
    return bias.reshape(DEPTH, H_A, NA_QBLOCKS, NA_QROWS * GRID_W, NA_WROWS * GRID_W)


def _lat_na(z, cache_k, cache_v, bias, l):
    zc = lambda off: pl.BlockSpec((DEC_SEQ, HEAD_DIM), lambda h, b: (b, off // HEAD_DIM + h))
    cache = pl.BlockSpec((None, None, None, PAST_LEN, HEAD_DIM), lambda h, b: (b, l, h, 0, 0))
    return pl.pallas_call(
        _na_kernel,
        grid=(H_A, DEC_BATCH),
        in_specs=[zc(Q_A), zc(K_A), zc(V_A), cache, cache,
                  pl.BlockSpec((None, None, NA_QBLOCKS, NA_QROWS * GRID_W, NA_WROWS * GRID_W),
                               lambda h, b: (l, h, 0, 0, 0))],
        out_specs=pl.BlockSpec((DEC_SEQ, HEAD_DIM), lambda h, b: (b, h)),
        out_shape=jax.ShapeDtypeStruct((N_LAT_TOK, W_A), BF16),
        compiler_params=_params("parallel", "parallel"),
        name="lat_na",
    )(z, z, z, cache_k, cache_v, bias)


def _rope(x, cos, sin_signed):
    lane = lax.broadcasted_iota(jnp.int32, x.shape, 1)
    partner = jnp.where(lane % (HEAD_DIM // 2) < HEAD_DIM // 4,
                        pltpu.roll(x, HEAD_DIM - HEAD_DIM // 4, 1),
                        pltpu.roll(x, HEAD_DIM // 4, 1))
    return x * cos + partner * sin_signed


def _rope_tables():
    half = HEAD_DIM // 2
    quarter = half // 2
    t = np.arange(DEC_SEQ)
    freqs = ROPE_BASE ** (-jnp.arange(quarter, dtype=F32) / quarter)

    def tables(pos):
        ang = jnp.asarray(pos, F32)[:, None] * freqs[None, :]
        cos, sin = jnp.cos(ang), jnp.sin(ang)
        return jnp.concatenate([cos, cos], axis=-1), jnp.concatenate([-sin, sin], axis=-1)

    cr, sr = tables(t // GRID_W)
    cc, sc = tables(t % GRID_W)
    return jnp.concatenate([cr, cc], axis=-1), jnp.concatenate([sr, sc], axis=-1)


def _win_kernel(l, sink_ref, q_ref, k_ref, v_ref, ck_ref, cv_ref, cos_ref, sin_ref, o_ref, kr_ref):
    g = pl.program_id(1)
    ck = ck_ref[...].astype(BF16)
    cv = _with_ones(cv_ref[...])
    v_aug = _with_ones(v_ref[...])
    kr_ref[...] = _rope(k_ref[...], cos_ref[...], sin_ref[...]).astype(BF16)
    sink = _sink_column(sink_ref, l * H_B + g * REP_B, BLK)
    n_blk = DEC_SEQ // BLK
    span = BLK + 2 * WIN

    def window_bias(first_key_offset):
        qpos = first_key_offset + lax.broadcasted_iota(jnp.int32, (REP_B * BLK, span), 0) % BLK
        kpos = lax.broadcasted_iota(jnp.int32, (REP_B * BLK, span), 1)
        return jnp.where(jnp.abs(qpos - kpos) <= WIN, 0.0, -jnp.inf).astype(F32)

    bias_first = window_bias(0)
    bias_rest = window_bias(BLK)
    def key_span(nb):
        return slice(max(nb - 1, 0) * BLK, min(nb + 2, n_blk) * BLK)

    def block_scores(nb):
        rows = slice(nb * BLK, (nb + 1) * BLK)
        span_nb = key_span(nb)
        cos = cos_ref[rows, :]
        sin = sin_ref[rows, :]
        q = jnp.concatenate(
            [_rope(q_ref[rows, r * HEAD_DIM:(r + 1) * HEAD_DIM], cos, sin) for r in range(REP_B)],
            axis=0).astype(BF16)
        bias = (bias_first if nb == 0 else bias_rest)[:, :span_nb.stop - span_nb.start]
        s_loc = _dot_nt(q, kr_ref[span_nb, :]) * ATTN_SCALE + bias
        s_ctx = _dot_nt(q, ck) * ATTN_SCALE
        return s_loc, s_ctx

    def block_output(nb, scores):
        rows = slice(nb * BLK, (nb + 1) * BLK)
        o = _softmax_pv(list(scores), [v_aug[key_span(nb), :], cv], sink).astype(BF16)
        for r in range(REP_B):
            o_ref[rows, r * HEAD_DIM:(r + 1) * HEAD_DIM] = o[r * BLK:(r + 1) * BLK]

    scores = [block_scores(nb) for nb in range(n_blk)]
    for nb in range(n_blk):
        block_output(nb, scores[nb])


def _lat_win(z, cache_k, cache_v, sink, cos, sin, l):
    zc = lambda off: pl.BlockSpec((DEC_SEQ, HEAD_DIM), lambda b, g: (b, off // HEAD_DIM + g))
    cache = pl.BlockSpec((None, None, None, PAST_LEN, HEAD_DIM), lambda b, g: (b, l, g, 0, 0))
    tab = pl.BlockSpec((DEC_SEQ, HEAD_DIM), lambda b, g: (0, 0))
    qw = REP_B * HEAD_DIM
    return pl.pallas_call(
        functools.partial(_win_kernel, l),
        grid=(DEC_BATCH, KV_B),
        in_specs=[pl.BlockSpec(memory_space=pltpu.SMEM),
                  pl.BlockSpec((DEC_SEQ, qw), lambda b, g: (b, Q_B // qw + g)),
                  zc(K_B), zc(V_B), cache, cache, tab, tab],
        out_specs=pl.BlockSpec((DEC_SEQ, qw), lambda b, g: (b, g)),
        out_shape=jax.ShapeDtypeStruct((N_LAT_TOK, W_B), BF16),
        scratch_shapes=[pltpu.VMEM((DEC_SEQ, HEAD_DIM), BF16)],
        compiler_params=_params("parallel", "parallel"),
        name="lat_win",
    )(sink, z, z, z, cache_k, cache_v, cos, sin)


def _gmlp_kernel(u_ref, v_ref, gw_ref, gb_ref, lng_ref, lnb_ref, o_ref):
    def write(n, g, val):
        o_ref[n * CHUNK:(n + 1) * CHUNK, g * C_GROUP:(g + 1) * C_GROUP] = val.astype(BF16)

    _gmlp(u_ref[...], v_ref[...], gw_ref, gb_ref, lng_ref, lnb_ref, write)


def _lat_gmlp(z, gmlp_w, gmlp_b4, ln_g, ln_b, l):
    rows = DEC_SEQ
    return pl.pallas_call(
        _gmlp_kernel,
        grid=(N_LAT_TOK // rows,),
        in_specs=[pl.BlockSpec((rows, W_C), lambda i: (i, U_C // W_C)),
                  pl.BlockSpec((rows, W_C), lambda i: (i, V_C // W_C))] + _gmlp_specs(l),
        out_specs=pl.BlockSpec((rows, W_C), lambda i: (i, 0)),
        out_shape=jax.ShapeDtypeStruct((N_LAT_TOK, W_C), BF16),
        compiler_params=_params("parallel"),
        name="lat_gmlp",
    )(z, z, gmlp_w, gmlp_b4, ln_g, ln_b)


EPI_ROWS = 128


def _residual_epilogue(y, rows, x_ref, g_post_ref, gate_ref, xo_ref, next_norm):
    x_new = x_ref[rows, :] + _rms(y, gate_ref[...] * g_post_ref[...])
    xo_ref[rows, :] = x_new
    if next_norm is not None:
        g_ref, sc_ref, sh_ref, h_ref = next_norm
        h_ref[rows, :] = (_rms(x_new, g_ref[...] * (1.0 + sc_ref[...])) + sh_ref[...]).astype(BF16)


def _out_kernel(widths, *refs):
    o_refs = refs[:len(widths)]
    (w_ref, x_ref, g_post_ref, gate_ref, g_next_ref, sc_ref, sh_ref,
     xo_ref, h_ref) = refs[len(widths):]
    for r in range(x_ref.shape[0] // EPI_ROWS):
        rows = slice(r * EPI_ROWS, (r + 1) * EPI_ROWS)
        y = None
        off = 0
        for o_ref, wd in zip(o_refs, widths):
            part = _dot(o_ref[rows, :], w_ref[off:off + wd, :])
            y = part if y is None else y + part
            off += wd
        _residual_epilogue(y, rows, x_ref, g_post_ref, gate_ref, xo_ref,
                           (g_next_ref, sc_ref, sh_ref, h_ref))


def _out_proj(o_list, x, mod5, g_post, g_ffn_pre, w_out, l, row_fn, tm):
    m = x.shape[0]
    widths = tuple(o.shape[1] for o in o_list)
    row_spec = pl.BlockSpec((tm, D_MODEL), lambda i: (i, 0))
    return pl.pallas_call(
        functools.partial(_out_kernel, widths),
        grid=(m // tm,),
        in_specs=[pl.BlockSpec((tm, wd), lambda i: (i, 0)) for wd in widths] + [
            pl.BlockSpec((None, MIX_WIDTH, D_MODEL), lambda i: (l, 0, 0)),
            row_spec,
            _vec_spec(l),
            _mod_spec(l, 2, row_fn),
            _vec_spec(l),
            _mod_spec(l, 4, row_fn),
            _mod_spec(l, 3, row_fn),
        ],
        out_specs=[row_spec, row_spec],
        out_shape=[jax.ShapeDtypeStruct((m, D_MODEL), F32),
                   jax.ShapeDtypeStruct((m, D_MODEL), BF16)],
        compiler_params=_params("parallel"),
        name="out_proj",
    )(*o_list, w_out, x, g_post, mod5, g_ffn_pre, mod5, mod5)


FFN_TC = 2 * MXU_TILE
FFN_ROWS = 512
FFN_LAST_ROWS = 256


def _ffn_kernel(with_next, h_ref, wg_ref, wu_ref, wd_ref, x_hbm, g_post_ref, gate_ref, *refs):
    if with_next:
        g_next_ref, sc_ref, sh_ref, xo_ref, ho_ref, x_buf, x_sem = refs
        next_norm = (g_next_ref, sc_ref, sh_ref, ho_ref)
    else:
        xo_ref, x_buf, x_sem = refs
        next_norm = None
    tm = x_buf.shape[0]
    c = pl.program_id(1)
    last = pl.num_programs(1) - 1

    def x_copy():
        row0 = pl.multiple_of(pl.program_id(0) * tm, tm)
        return pltpu.make_async_copy(x_hbm.at[pl.ds(row0, tm), :], x_buf, x_sem)

    def chunk_product(rows):
        h = h_ref[rows, :]
        a = (jax.nn.silu(_dot(h, wg_ref[...])) * _dot(h, wu_ref[...])).astype(BF16)
        return _dot(a, wd_ref[...])

    def row_blocks(n):
        for r in range(tm // n):
            yield slice(r * n, (r + 1) * n)

    @pl.when(c == 0)
    def _():
        x_copy().start()
        for rows in row_blocks(FFN_ROWS):
            xo_ref[rows, :] = chunk_product(rows)

    @pl.when((c > 0) & (c < last))
    def _():
        for rows in row_blocks(FFN_ROWS):
            xo_ref[rows, :] += chunk_product(rows)

    @pl.when(c == last)
    def _():
        x_copy().wait()
        for rows in row_blocks(FFN_LAST_ROWS):
            y = xo_ref[rows, :] + chunk_product(rows)
            _residual_epilogue(y, rows, x_buf, g_post_ref, gate_ref, xo_ref, next_norm)


def _ffn(h, x, mod5, g_post, g_mix_pre, w_gate, w_up, w_down, l, row_fn, tm):
    m = x.shape[0]
    with_next = l + 1 < DEPTH
    row_spec = pl.BlockSpec((tm, D_MODEL), lambda i, c: (i, 0))
    up_spec = pl.BlockSpec((None, D_MODEL, FFN_TC), lambda i, c: (l, 0, c))
    in_specs = [
        row_spec, up_spec, up_spec,
        pl.BlockSpec((None, FFN_TC, D_MODEL), lambda i, c: (l, c, 0)),
        pl.BlockSpec(memory_space=pl.ANY),
        _vec_spec(l),
        _mod_spec(l, 5, row_fn),
    ]
    args = [h, w_gate, w_up, w_down, x, g_post, mod5]
    out_specs = [row_spec]
    out_shape = [jax.ShapeDtypeStruct((m, D_MODEL), F32)]
    if with_next:
        in_specs += [_vec_spec(l + 1), _mod_spec(l + 1, 1, row_fn), _mod_spec(l + 1, 0, row_fn)]
        args += [g_mix_pre, mod5, mod5]
        out_specs.append(row_spec)
        out_shape.append(jax.ShapeDtypeStruct((m, D_MODEL), BF16))
    outs = pl.pallas_call(
        functools.partial(_ffn_kernel, with_next),
        grid=(m // tm, D_FF // FFN_TC),
        in_specs=in_specs,
        out_specs=out_specs,
        out_shape=out_shape,
        scratch_shapes=[pltpu.VMEM((tm, D_MODEL), F32), pltpu.SemaphoreType.DMA(())],
        compiler_params=_params("parallel", "arbitrary"),
        name="ffn",
    )(*args)
    return outs if with_next else (outs[0], None)


def kernel(x_prompt, x_sample, cache_a_k, cache_a_v, cache_b_k, cache_b_v, c, c_ctx,
           mod_w, mod_b, norm_mix_pre, norm_mix_post, norm_ffn_pre, norm_ffn_post,
           w_in, w_out, rpb_a, sink_b, gmlp_ln_g, gmlp_ln_b, gmlp_w, gmlp_b,
           w_gate, w_up, w_down):
    cond = jnp.zeros((MOD_ROWS, D_MODEL), F32)
    cond = cond.at[:DEC_BATCH].set(c).at[CTX_MOD_ROW].set(c_ctx)
    mod5 = _modulation(cond, mod_w, mod_b).reshape(DEPTH, MOD_ROWS, N_MOD, 1, D_MODEL)

    vec = lambda a: a.reshape(DEPTH, 1, -1)
    g_mix_pre, g_mix_post = vec(norm_mix_pre), vec(norm_mix_post)
    g_ffn_pre, g_ffn_post = vec(norm_ffn_pre), vec(norm_ffn_post)
    ln_g, ln_b = vec(gmlp_ln_g), vec(gmlp_ln_b)
    gmlp_b4 = gmlp_b.reshape(DEPTH, G_C, CHUNK, 1)
    sink = sink_b.reshape(DEPTH * H_B)
    w_in_b, w_out_b = w_in.astype(BF16), w_out.astype(BF16)
    w_gate_b, w_up_b, w_down_b = w_gate.astype(BF16), w_up.astype(BF16), w_down.astype(BF16)
    bias = _na_bias(rpb_a)
    cos, sin = _rope_tables()

    tm_wide, tm_row = 1024, 512
    ctx_row = lambda i: CTX_MOD_ROW
    lat_row = lambda tm: (lambda i: (i * tm) // DEC_SEQ)

    xp = x_prompt.reshape(N_CTX_TOK, D_MODEL)
    xs = x_sample.reshape(N_LAT_TOK, D_MODEL)
    hp = _norm_mod(xp, mod5, g_mix_pre, 0, ctx_row, tm_row)
    hs = _norm_mod(xs, mod5, g_mix_pre, 0, lat_row(tm_row), tm_row)
    kv_all = None
    for l in range(DEPTH):
        o, *kv_all = _ctx_mix(hp, w_in_b, sink, gmlp_w, gmlp_b4, ln_g, ln_b, kv_all, l)
        xp, hp = _out_proj([o], xp, mod5, g_mix_post, g_ffn_pre, w_out_b, l, ctx_row, tm_row)
        xp, hp = _ffn(hp, xp, mod5, g_ffn_post, g_mix_pre, w_gate_b, w_up_b, w_down_b, l,
                      ctx_row, tm_wide)
        z = _in_proj(hs, w_in_b, l, tm_wide)
        o_a = _lat_na(z, cache_a_k, cache_a_v, bias, l)
        o_b = _lat_win(z, cache_b_k, cache_b_v, sink, cos, sin, l)
        o_c = _lat_gmlp(z, gmlp_w, gmlp_b4, ln_g, ln_b, l)
        xs, hs = _out_proj([o_a, o_b, o_c], xs, mod5, g_mix_post, g_ffn_pre, w_out_b, l,
                           lat_row(tm_row), tm_row)
        xs, hs = _ffn(hs, xs, mod5, g_ffn_post, g_mix_pre, w_gate_b, w_up_b, w_down_b, l,
                      lat_row(tm_wide), tm_wide)

    return (xp.reshape(BATCH, SEQ, D_MODEL), xs.reshape(DEC_BATCH, DEC_SEQ, D_MODEL), *kv_all)
```

```python
import functools

import jax
import jax.numpy as jnp
import numpy as np
from jax import lax
from jax.experimental import pallas as pl
from jax.experimental.pallas import tpu as pltpu

D_MODEL = 2048
BATCH = 32
SEQ = 256
DEPTH = 4
DEC_BATCH = 8
DEC_SEQ = 1024
PAST_LEN = 256
GRID_W = 64
GRID_ROWS = DEC_SEQ // GRID_W
HEAD_DIM = 128
H_A = 6
H_B = 6
KV_B = 2
REP_B = H_B // KV_B
G_C = 4
C_GROUP = 128
W_A = H_A * HEAD_DIM
W_B = H_B * HEAD_DIM
W_C = G_C * C_GROUP
MIX_WIDTH = W_A + W_B + W_C
NA_ROWS = 8
NA_COLS = 16
WIN = 128
BLK = 128
CHUNK = 128
D_FF = 5632
N_MOD = 6
ROPE_BASE = 10000.0
NORM_EPS = 1e-6
ATTN_SCALE = HEAD_DIM ** -0.5
Q_A = 0
K_A = Q_A + W_A
V_A = K_A + W_A
Q_B = V_A + W_A
K_B = Q_B + W_B
V_B = K_B + KV_B * HEAD_DIM
U_C = V_B + KV_B * HEAD_DIM
V_C = U_C + W_C
IN_WIDTH = V_C + W_C

N_CTX_TOK = BATCH * SEQ
N_LAT_TOK = DEC_BATCH * DEC_SEQ
MOD_ROWS = 16
CTX_MOD_ROW = DEC_BATCH

VMEM_LIMIT_BYTES = 58 * 1024 * 1024
MXU_TILE = 256

BF16 = jnp.bfloat16
F32 = jnp.float32


def _params(*sem):
    return pltpu.CompilerParams(dimension_semantics=sem, vmem_limit_bytes=VMEM_LIMIT_BYTES)


def _dot(a, b):
    return jnp.dot(a, b, preferred_element_type=F32)


def _dot_nt(a, b):
    return lax.dot_general(a, b, (((1,), (1,)), ((), ())), preferred_element_type=F32)


def _rms(x, g):
    return (x * lax.rsqrt(jnp.mean(x * x, axis=-1, keepdims=True) + NORM_EPS)) * g


def _mod_kernel(c_ref, w_ref, b_ref, o_ref):
    s = jax.nn.silu(c_ref[...])
    o_ref[...] = _dot(s.astype(BF16), w_ref[...].astype(BF16)) + b_ref[...]


def _modulation(cond, mod_w, mod_b):
    tn = 1024
    n = N_MOD * D_MODEL
    return pl.pallas_call(
        _mod_kernel,
        grid=(DEPTH, n // tn),
        in_specs=[
            pl.BlockSpec((MOD_ROWS, D_MODEL), lambda l, j: (0, 0)),
            pl.BlockSpec((None, D_MODEL, tn), lambda l, j: (l, 0, j)),
            pl.BlockSpec((None, 1, tn), lambda l, j: (l, 0, j)),
        ],
        out_specs=pl.BlockSpec((None, MOD_ROWS, tn), lambda l, j: (l, 0, j)),
        out_shape=jax.ShapeDtypeStruct((DEPTH, MOD_ROWS, n), F32),
        compiler_params=_params("parallel", "parallel"),
        name="modulation",
    )(cond, mod_w, mod_b.reshape(DEPTH, 1, n))


def _mod_spec(l, which, row_fn):
    return pl.BlockSpec((None, None, None, 1, D_MODEL),
                        lambda i, *_: (l, row_fn(i), which, 0, 0))


def _vec_spec(l):
    return pl.BlockSpec((None, 1, D_MODEL), lambda i, *_: (l, 0, 0))


def _modulated_norm(x, g, scale, shift):
    return (_rms(x, g) * (1.0 + scale) + shift).astype(BF16)


def _norm_kernel(x_ref, g_ref, sc_ref, sh_ref, h_ref):
    h_ref[...] = _modulated_norm(x_ref[...], g_ref[...], sc_ref[...], sh_ref[...])


def _norm_mod(x, mod5, g_pre, l, row_fn, tm):
    m = x.shape[0]
    return pl.pallas_call(
        _norm_kernel,
        grid=(m // tm,),
        in_specs=[pl.BlockSpec((tm, D_MODEL), lambda i: (i, 0)),
                  _vec_spec(l), _mod_spec(l, 1, row_fn), _mod_spec(l, 0, row_fn)],
        out_specs=pl.BlockSpec((tm, D_MODEL), lambda i: (i, 0)),
        out_shape=jax.ShapeDtypeStruct((m, D_MODEL), BF16),
        compiler_params=_params("parallel"),
        name="norm_mod",
    )(x, g_pre, mod5, mod5)


def _in_kernel(h_ref, w_ref, z_ref):
    z_ref[...] = _dot(h_ref[...], w_ref[...])


def _in_proj(h, w_in, l, tm):
    m = h.shape[0]
    tn = 6 * MXU_TILE
    return pl.pallas_call(
        _in_kernel,
        grid=(IN_WIDTH // tn, m // tm),
        in_specs=[
            pl.BlockSpec((tm, D_MODEL), lambda j, i: (i, 0)),
            pl.BlockSpec((None, D_MODEL, tn), lambda j, i: (l, 0, j)),
        ],
        out_specs=pl.BlockSpec((tm, tn), lambda j, i: (i, j)),
        out_shape=jax.ShapeDtypeStruct((m, IN_WIDTH), F32),
        compiler_params=_params("parallel", "parallel"),
        name="in_proj",
    )(h, w_in)


def _with_ones(v):
    return jnp.concatenate([v.astype(BF16), jnp.ones(v.shape, BF16)], axis=-1)


def _softmax_pv(scores, values, extra_logit=None):
    m = scores[0].max(axis=-1, keepdims=True)
    for s in scores[1:]:
        m = jnp.maximum(m, s.max(axis=-1, keepdims=True))
    if extra_logit is not None:
        m = jnp.maximum(m, extra_logit)
    acc = None
    for s, v in zip(scores, values):
        a = _dot(jnp.exp(s - m).astype(BF16), v)
        acc = a if acc is None else acc + a
    num, den = acc[:, :HEAD_DIM], acc[:, HEAD_DIM:]
    if extra_logit is not None:
        den = den + jnp.exp(extra_logit - m)
    return num / den


def _sink_column(sink_ref, base, rows_per_head):
    rows = lax.broadcasted_iota(jnp.int32, (REP_B * rows_per_head, 1), 0)
    col = jnp.full((REP_B * rows_per_head, 1), sink_ref[base], F32)
    for r in range(1, REP_B):
        col = jnp.where(rows >= r * rows_per_head, sink_ref[base + r], col)
    return col


def _gmlp(u_raw, v_raw, gw_ref, gb_ref, lng_ref, lnb_ref, write):
    u = jax.nn.gelu(u_raw)
    v = jax.nn.gelu(v_raw)
    mu = jnp.mean(v, axis=-1, keepdims=True)
    xc = v - mu
    vn = xc * lax.rsqrt(jnp.mean(xc * xc, axis=-1, keepdims=True) + NORM_EPS)
    vn = (vn * lng_ref[...] + lnb_ref[...]).astype(BF16)
    for n in range(u_raw.shape[0] // CHUNK):
        rs = slice(n * CHUNK, (n + 1) * CHUNK)
        for g in range(G_C):
            cs = slice(g * C_GROUP, (g + 1) * C_GROUP)
            s = _dot(gw_ref[g].astype(BF16), vn[rs, cs]) + gb_ref[g]
            write(n, g, u[rs, cs] * s)


CTX_PROJ_COLS = 2 * MXU_TILE
def _ctx_seqs(l):
    return 1 if l == 0 else 2


def _ctx_mix_kernel(l, sink_ref, h_ref, w_ref, gw_ref, gb_ref, lng_ref, lnb_ref, *refs):
    o_ref, ka_ref, va_ref, kb_ref, vb_ref, z_ref = refs[-6:]
    kv_refs = (ka_ref, va_ref, kb_ref, vb_ref)
    if l == 0:
        for ref in kv_refs:
            ref[:, 1:] = jnp.zeros((ref.shape[0], DEPTH - 1) + ref.shape[2:], F32)
        kv_refs = tuple(ref.at[:, 0] for ref in kv_refs)
    for s in range(_ctx_seqs(l)):
        _ctx_mix_sequence(l, s, sink_ref, h_ref, w_ref, gw_ref, gb_ref, lng_ref, lnb_ref,
                          o_ref, *kv_refs, z_ref)


def _ctx_mix_sequence(l, s, sink_ref, h_ref, w_ref, gw_ref, gb_ref, lng_ref, lnb_ref,
                      o_ref, ka_ref, va_ref, kb_ref, vb_ref, z_ref):
    tok = slice(s * SEQ, (s + 1) * SEQ)
    h_in = h_ref[tok, :]
    for c0 in range(0, IN_WIDTH, CTX_PROJ_COLS):
        z_ref[tok, c0:c0 + CTX_PROJ_COLS] = _dot(h_in, w_ref[:, c0:c0 + CTX_PROJ_COLS])

    def col(off, i=0):
        return z_ref[tok, off + i * HEAD_DIM: off + (i + 1) * HEAD_DIM]

    for h in range(H_A):
        k = col(K_A, h)
        v = col(V_A, h)
        ka_ref[s, h] = k
        va_ref[s, h] = v
        sc = _dot_nt(col(Q_A, h).astype(BF16), k.astype(BF16)) * ATTN_SCALE
        o_ref[tok, h * HEAD_DIM:(h + 1) * HEAD_DIM] = _softmax_pv([sc], [_with_ones(v)]).astype(BF16)

    for g in range(KV_B):
        k = col(K_B, g)
        v = col(V_B, g)
        kb_ref[s, g] = k
        vb_ref[s, g] = v
        q = jnp.concatenate([col(Q_B, g * REP_B + r) for r in range(REP_B)], axis=0).astype(BF16)
        sc = _dot_nt(q, k.astype(BF16)) * ATTN_SCALE
        sink = _sink_column(sink_ref, l * H_B + g * REP_B, SEQ)
        o = _softmax_pv([sc], [_with_ones(v)], sink).astype(BF16)
        for r in range(REP_B):
            c0 = W_A + (g * REP_B + r) * HEAD_DIM
            o_ref[s * SEQ:(s + 1) * SEQ, c0:c0 + HEAD_DIM] = o[r * SEQ:(r + 1) * SEQ]

    def write(n, g, val):
        c0 = W_A + W_B + g * C_GROUP
        r0 = s * SEQ + n * CHUNK
        o_ref[r0:r0 + CHUNK, c0:c0 + C_GROUP] = val.astype(BF16)

    _gmlp(z_ref[tok, U_C:V_C], z_ref[tok, V_C:IN_WIDTH], gw_ref, gb_ref, lng_ref, lnb_ref, write)


def _gmlp_specs(l):
    return [
        pl.BlockSpec((None, G_C, CHUNK, CHUNK), lambda i, *_: (l, 0, 0, 0)),
        pl.BlockSpec((None, G_C, CHUNK, 1), lambda i, *_: (l, 0, 0, 0)),
        pl.BlockSpec((None, 1, W_C), lambda i, *_: (l, 0, 0)),
        pl.BlockSpec((None, 1, W_C), lambda i, *_: (l, 0, 0)),
    ]


def _ctx_mix(h, w_in, sink, gmlp_w, gmlp_b4, ln_g, ln_b, kv_all, l):
    seqs = _ctx_seqs(l)
    rows = seqs * SEQ
    kv_heads = (H_A, H_A, KV_B, KV_B)
    if l == 0:
        kv_spec = lambda nh: pl.BlockSpec((seqs, DEPTH, nh, SEQ, HEAD_DIM), lambda b: (b, 0, 0, 0, 0))
        kv_in, aliases = [], {}
    else:
        kv_spec = lambda nh: pl.BlockSpec((seqs, None, nh, SEQ, HEAD_DIM), lambda b: (b, l, 0, 0, 0))
        kv_in = list(kv_all)
        n_in = 7
        aliases = {n_in + i: 1 + i for i in range(4)}
    return pl.pallas_call(
        functools.partial(_ctx_mix_kernel, l),
        grid=(BATCH // seqs,),
        in_specs=[
            pl.BlockSpec(memory_space=pltpu.SMEM),
            pl.BlockSpec((rows, D_MODEL), lambda b: (b, 0)),
            pl.BlockSpec((None, D_MODEL, IN_WIDTH), lambda b: (l, 0, 0), pipeline_mode=pl.Buffered(1)),
        ] + _gmlp_specs(l) + [pl.BlockSpec(memory_space=pl.ANY)] * len(kv_in),
        out_specs=[pl.BlockSpec((rows, MIX_WIDTH), lambda b: (b, 0))] + [kv_spec(nh) for nh in kv_heads],
        out_shape=[jax.ShapeDtypeStruct((N_CTX_TOK, MIX_WIDTH), BF16)]
                  + [jax.ShapeDtypeStruct((BATCH, DEPTH, nh, SEQ, HEAD_DIM), F32) for nh in kv_heads],
        scratch_shapes=[pltpu.VMEM((rows, IN_WIDTH), F32)],
        input_output_aliases=aliases,
        compiler_params=_params("parallel"),
        name="ctx_mix",
    )(sink, h, w_in, gmlp_w, gmlp_b4, ln_g, ln_b, *kv_in)


NA_QROWS = 4
NA_WROWS = 12
NA_QBLOCKS = GRID_ROWS // NA_QROWS


def _na_band_start(r):
    return min(max(r - NA_ROWS // 2, 0), GRID_ROWS - NA_ROWS)


def _na_window_start(qb):
    return min(max(qb * NA_QROWS - NA_ROWS // 2, 0), GRID_ROWS - NA_WROWS)


def _na_kernel(q_ref, k_ref, v_ref, ck_ref, cv_ref, bias_ref, o_ref):
    ck = ck_ref[...].astype(BF16)
    cv = _with_ones(cv_ref[...])
    v_aug = _with_ones(v_ref[...])

    def window(qb):
        lo = _na_band_start(qb * NA_QROWS) * GRID_W
        hi = (_na_band_start((qb + 1) * NA_QROWS - 1) + NA_ROWS) * GRID_W
        return slice(lo, -(-hi // HEAD_DIM) * HEAD_DIM)

    def bias_lanes(qb):
        first = _na_window_start(qb) * GRID_W
        keys = window(qb)
        return slice(keys.start - first, keys.stop - first)

    def query_rows(qb):
        return slice(qb * NA_QROWS * GRID_W, (qb + 1) * NA_QROWS * GRID_W)

    scores = []
    for qb in range(NA_QBLOCKS):
        q = q_ref[query_rows(qb), :].astype(BF16)
        s_loc = (_dot_nt(q, k_ref[window(qb), :].astype(BF16)) * ATTN_SCALE
                 + bias_ref[qb, :, bias_lanes(qb)])
        s_ctx = _dot_nt(q, ck) * ATTN_SCALE
        scores.append((s_loc, s_ctx))
    for qb in range(NA_QBLOCKS):
        o = _softmax_pv(list(scores[qb]), [v_aug[window(qb), :], cv])
        o_ref[query_rows(qb), :] = o.astype(BF16)


def _na_bias(rpb):
    n_dc = 2 * NA_COLS - 1
    edge = GRID_W - NA_COLS
    ext = jnp.concatenate([jnp.repeat(rpb[..., :1], edge, axis=-1), rpb,
                           jnp.repeat(rpb[..., -1:], edge, axis=-1)], axis=-1).astype(F32)
    period = 2 * GRID_W - 1
    reps = -(-(GRID_W * 2 * GRID_W) // period)
    shifted = jnp.tile(ext, reps)[..., :GRID_W * 2 * GRID_W].reshape(ext.shape[:-1] + (GRID_W, 2 * GRID_W))
    toep = shifted[..., ::-1, :GRID_W]
    col = np.arange(GRID_W)
    c0 = np.clip(col - NA_COLS // 2, 0, GRID_W - NA_COLS)
    col_ok = (col[None, :] >= c0[:, None]) & (col[None, :] < c0[:, None] + NA_COLS)
    toep = jnp.where(col_ok, toep, -jnp.inf)
    by_q = toep.transpose(0, 1, 3, 2, 4).reshape(DEPTH, H_A, GRID_W, (2 * NA_ROWS - 1) * GRID_W)
    per_row = []
    for r in range(GRID_ROWS):
        w0 = _na_window_start(r // NA_QROWS)
        r0 = _na_band_start(r)
        a0 = r0 - r + NA_ROWS - 1
        band = by_q[..., a0 * GRID_W:(a0 + NA_ROWS) * GRID_W]
        left = (r0 - w0) * GRID_W
        right = (NA_WROWS - NA_ROWS) * GRID_W - left
        per_row.append(jnp.pad(band, ((0, 0), (0, 0), (0, 0), (left, right)),
                               constant_values=-jnp.inf))
    bias = jnp.concatenate(per_row, axis=2)
    return bias.reshape(DEPTH, H_A, NA_QBLOCKS, NA_QROWS * GRID_W, NA_WROWS * GRID_W)


def _lat_na(z, cache_k, cache_v, bias, l):
    zc = lambda off: pl.BlockSpec((DEC_SEQ, HEAD_DIM), lambda h, b: (b, off // HEAD_DIM + h))
    cache = pl.BlockSpec((None, None, None, PAST_LEN, HEAD_DIM), lambda h, b: (b, l, h, 0, 0))
    return pl.pallas_call(
        _na_kernel,
        grid=(H_A, DEC_BATCH),
        in_specs=[zc(Q_A), zc(K_A), zc(V_A), cache, cache,
                  pl.BlockSpec((None, None, NA_QBLOCKS, NA_QROWS * GRID_W, NA_WROWS * GRID_W),
                               lambda h, b: (l, h, 0, 0, 0))],
        out_specs=pl.BlockSpec((DEC_SEQ, HEAD_DIM), lambda h, b: (b, h)),
        out_shape=jax.ShapeDtypeStruct((N_LAT_TOK, W_A), BF16),
        compiler_params=_params("parallel", "parallel"),
        name="lat_na",
    )(z, z, z, cache_k, cache_v, bias)


def _rope(x, cos, sin_signed):
    lane = lax.broadcasted_iota(jnp.int32, x.shape, 1)
    partner = jnp.where(lane % (HEAD_DIM // 2) < HEAD_DIM // 4,
                        pltpu.roll(x, HEAD_DIM - HEAD_DIM // 4, 1),
                        pltpu.roll(x, HEAD_DIM // 4, 1))
    return x * cos + partner * sin_signed


def _rope_tables():
    half = HEAD_DIM // 2
    quarter = half // 2
    t = np.arange(DEC_SEQ)
    freqs = ROPE_BASE ** (-jnp.arange(quarter, dtype=F32) / quarter)

    def tables(pos):
        ang = jnp.asarray(pos, F32)[:, None] * freqs[None, :]
        cos, sin = jnp.cos(ang), jnp.sin(ang)
        return jnp.concatenate([cos, cos], axis=-1), jnp.concatenate([-sin, sin], axis=-1)

    cr, sr = tables(t // GRID_W)
    cc, sc = tables(t % GRID_W)
    return jnp.concatenate([cr, cc], axis=-1), jnp.concatenate([sr, sc], axis=-1)


def _win_kernel(l, sink_ref, q_ref, k_ref, v_ref, ck_ref, cv_ref, cos_ref, sin_ref, o_ref, kr_ref):
    g = pl.program_id(1)
    ck = ck_ref[...].astype(BF16)
    cv = _with_ones(cv_ref[...])
    v_aug = _with_ones(v_ref[...])
    kr_ref[...] = _rope(k_ref[...], cos_ref[...], sin_ref[...]).astype(BF16)
    sink = _sink_column(sink_ref, l * H_B + g * REP_B, BLK)
    n_blk = DEC_SEQ // BLK
    span = BLK + 2 * WIN

    def window_bias(first_key_offset):
        qpos = first_key_offset + lax.broadcasted_iota(jnp.int32, (REP_B * BLK, span), 0) % BLK
        kpos = lax.broadcasted_iota(jnp.int32, (REP_B * BLK, span), 1)
        return jnp.where(jnp.abs(qpos - kpos) <= WIN, 0.0, -jnp.inf).astype(F32)

    bias_first = window_bias(0)
    bias_rest = window_bias(BLK)
    def key_span(nb):
        return slice(max(nb - 1, 0) * BLK, min(nb + 2, n_blk) * BLK)

    def block_scores(nb):
        rows = slice(nb * BLK, (nb + 1) * BLK)
        span_nb = key_span(nb)
        cos = cos_ref[rows, :]
        sin = sin_ref[rows, :]
        q = jnp.concatenate(
            [_rope(q_ref[rows, r * HEAD_DIM:(r + 1) * HEAD_DIM], cos, sin) for r in range(REP_B)],
            axis=0).astype(BF16)
        bias = (bias_first if nb == 0 else bias_rest)[:, :span_nb.stop - span_nb.start]
        s_loc = _dot_nt(q, kr_ref[span_nb, :]) * ATTN_SCALE + bias
        s_ctx = _dot_nt(q, ck) * ATTN_SCALE
        return s_loc, s_ctx

    def block_output(nb, scores):
        rows = slice(nb * BLK, (nb + 1) * BLK)
        o = _softmax_pv(list(scores), [v_aug[key_span(nb), :], cv], sink).astype(BF16)
        for r in range(REP_B):
            o_ref[rows, r * HEAD_DIM:(r + 1) * HEAD_DIM] = o[r * BLK:(r + 1) * BLK]

    scores = [block_scores(nb) for nb in range(n_blk)]
    for nb in range(n_blk):
        block_output(nb, scores[nb])


def _lat_win(z, cache_k, cache_v, sink, cos, sin, l):
    zc = lambda off: pl.BlockSpec((DEC_SEQ, HEAD_DIM), lambda b, g: (b, off // HEAD_DIM + g))
    cache = pl.BlockSpec((None, None, None, PAST_LEN, HEAD_DIM), lambda b, g: (b, l, g, 0, 0))
    tab = pl.BlockSpec((DEC_SEQ, HEAD_DIM), lambda b, g: (0, 0))
    qw = REP_B * HEAD_DIM
    return pl.pallas_call(
        functools.partial(_win_kernel, l),
        grid=(DEC_BATCH, KV_B),
        in_specs=[pl.BlockSpec(memory_space=pltpu.SMEM),
                  pl.BlockSpec((DEC_SEQ, qw), lambda b, g: (b, Q_B // qw + g)),
                  zc(K_B), zc(V_B), cache, cache, tab, tab],
        out_specs=pl.BlockSpec((DEC_SEQ, qw), lambda b, g: (b, g)),
        out_shape=jax.ShapeDtypeStruct((N_LAT_TOK, W_B), BF16),
        scratch_shapes=[pltpu.VMEM((DEC_SEQ, HEAD_DIM), BF16)],
        compiler_params=_params("parallel", "parallel"),
        name="lat_win",
    )(sink, z, z, z, cache_k, cache_v, cos, sin)


def _gmlp_kernel(u_ref, v_ref, gw_ref, gb_ref, lng_ref, lnb_ref, o_ref):
    def write(n, g, val):
        o_ref[n * CHUNK:(n + 1) * CHUNK, g * C_GROUP:(g + 1) * C_GROUP] = val.astype(BF16)

    _gmlp(u_ref[...], v_ref[...], gw_ref, gb_ref, lng_ref, lnb_ref, write)


def _lat_gmlp(z, gmlp_w, gmlp_b4, ln_g, ln_b, l):
    rows = DEC_SEQ
    return pl.pallas_call(
        _gmlp_kernel,
        grid=(N_LAT_TOK // rows,),
        in_specs=[pl.BlockSpec((rows, W_C), lambda i: (i, U_C // W_C)),
                  pl.BlockSpec((rows, W_C), lambda i: (i, V_C // W_C))] + _gmlp_specs(l),
        out_specs=pl.BlockSpec((rows, W_C), lambda i: (i, 0)),
        out_shape=jax.ShapeDtypeStruct((N_LAT_TOK, W_C), BF16),
        compiler_params=_params("parallel"),
        name="lat_gmlp",
    )(z, z, gmlp_w, gmlp_b4, ln_g, ln_b)


EPI_ROWS = 128


def _residual_epilogue(y, rows, x_ref, g_post_ref, gate_ref, xo_ref, next_norm):
    x_new = x_ref[rows, :] + _rms(y, gate_ref[...] * g_post_ref[...])
    xo_ref[rows, :] = x_new
    if next_norm is not None:
        g_ref, sc_ref, sh_ref, h_ref = next_norm
        h_ref[rows, :] = (_rms(x_new, g_ref[...] * (1.0 + sc_ref[...])) + sh_ref[...]).astype(BF16)


def _out_kernel(widths, *refs):
    o_refs = refs[:len(widths)]
    (w_ref, x_ref, g_post_ref, gate_ref, g_next_ref, sc_ref, sh_ref,
     xo_ref, h_ref) = refs[len(widths):]
    for r in range(x_ref.shape[0] // EPI_ROWS):
        rows = slice(r * EPI_ROWS, (r + 1) * EPI_ROWS)
        y = None
        off = 0
        for o_ref, wd in zip(o_refs, widths):
            part = _dot(o_ref[rows, :], w_ref[off:off + wd, :])
            y = part if y is None else y + part
            off += wd
        _residual_epilogue(y, rows, x_ref, g_post_ref, gate_ref, xo_ref,
                           (g_next_ref, sc_ref, sh_ref, h_ref))


def _out_proj(o_list, x, mod5, g_post, g_ffn_pre, w_out, l, row_fn, tm):
    m = x.shape[0]
    widths = tuple(o.shape[1] for o in o_list)
    row_spec = pl.BlockSpec((tm, D_MODEL), lambda i: (i, 0))
    return pl.pallas_call(
        functools.partial(_out_kernel, widths),
        grid=(m // tm,),
        in_specs=[pl.BlockSpec((tm, wd), lambda i: (i, 0)) for wd in widths] + [
            pl.BlockSpec((None, MIX_WIDTH, D_MODEL), lambda i: (l, 0, 0)),
            row_spec,
            _vec_spec(l),
            _mod_spec(l, 2, row_fn),
            _vec_spec(l),
            _mod_spec(l, 4, row_fn),
            _mod_spec(l, 3, row_fn),
        ],
        out_specs=[row_spec, row_spec],
        out_shape=[jax.ShapeDtypeStruct((m, D_MODEL), F32),
                   jax.ShapeDtypeStruct((m, D_MODEL), BF16)],
        compiler_params=_params("parallel"),
        name="out_proj",
    )(*o_list, w_out, x, g_post, mod5, g_ffn_pre, mod5, mod5)


FFN_TC = 2 * MXU_TILE
FFN_ROWS = 512
FFN_LAST_ROWS = 256


def _ffn_kernel(with_next, h_ref, wg_ref, wu_ref, wd_ref, x_hbm, g_post_ref, gate_ref, *refs):
    if with_next:
        g_next_ref, sc_ref, sh_ref, xo_ref, ho_ref, x_buf, x_sem = refs
        next_norm = (g_next_ref, sc_ref, sh_ref, ho_ref)
    else:
        xo_ref, x_buf, x_sem = refs
        next_norm = None
    tm = x_buf.shape[0]
    c = pl.program_id(1)
    last = pl.num_programs(1) - 1

    def x_copy():
        row0 = pl.multiple_of(pl.program_id(0) * tm, tm)
        return pltpu.make_async_copy(x_hbm.at[pl.ds(row0, tm), :], x_buf, x_sem)

    def row_blocks(n):
        return [slice(r * n, (r + 1) * n) for r in range(tm // n)]

    def chunk_products(n):
        acts = []
        for rows in row_blocks(n):
            h = h_ref[rows, :]
            acts.append((jax.nn.silu(_dot(h, wg_ref[...])) * _dot(h, wu_ref[...])).astype(BF16))
        for rows, a in zip(row_blocks(n), acts):
            yield rows, _dot(a, wd_ref[...])

    @pl.when(c == 0)
    def _():
        x_copy().start()
        for rows, y in chunk_products(FFN_ROWS):
            xo_ref[rows, :] = y

    @pl.when((c > 0) & (c < last))
    def _():
        for rows, y in chunk_products(FFN_ROWS):
            xo_ref[rows, :] += y

    @pl.when(c == last)
    def _():
        x_copy().wait()
        for rows, y in chunk_products(FFN_LAST_ROWS):
            _residual_epilogue(xo_ref[rows, :] + y, rows, x_buf, g_post_ref, gate_ref, xo_ref,
                               next_norm)


def _ffn(h, x, mod5, g_post, g_mix_pre, w_gate, w_up, w_down, l, row_fn, tm):
    m = x.shape[0]
    with_next = l + 1 < DEPTH
    row_spec = pl.BlockSpec((tm, D_MODEL), lambda i, c: (i, 0))
    up_spec = pl.BlockSpec((None, D_MODEL, FFN_TC), lambda i, c: (l, 0, c))
    in_specs = [
        row_spec, up_spec, up_spec,
        pl.BlockSpec((None, FFN_TC, D_MODEL), lambda i, c: (l, c, 0)),
        pl.BlockSpec(memory_space=pl.ANY),
        _vec_spec(l),
        _mod_spec(l, 5, row_fn),
    ]
    args = [h, w_gate, w_up, w_down, x, g_post, mod5]
    out_specs = [row_spec]
    out_shape = [jax.ShapeDtypeStruct((m, D_MODEL), F32)]
    if with_next:
        in_specs += [_vec_spec(l + 1), _mod_spec(l + 1, 1, row_fn), _mod_spec(l + 1, 0, row_fn)]
        args += [g_mix_pre, mod5, mod5]
        out_specs.append(row_spec)
        out_shape.append(jax.ShapeDtypeStruct((m, D_MODEL), BF16))
    outs = pl.pallas_call(
        functools.partial(_ffn_kernel, with_next),
        grid=(m // tm, D_FF // FFN_TC),
        in_specs=in_specs,
        out_specs=out_specs,
        out_shape=out_shape,
        scratch_shapes=[pltpu.VMEM((tm, D_MODEL), F32), pltpu.SemaphoreType.DMA(())],
        compiler_params=_params("parallel", "arbitrary"),
        name="ffn",
    )(*args)
    return outs if with_next else (outs[0], None)


def kernel(x_prompt, x_sample, cache_a_k, cache_a_v, cache_b_k, cache_b_v, c, c_ctx,
           mod_w, mod_b, norm_mix_pre, norm_mix_post, norm_ffn_pre, norm_ffn_post,
           w_in, w_out, rpb_a, sink_b, gmlp_ln_g, gmlp_ln_b, gmlp_w, gmlp_b,
           w_gate, w_up, w_down):
    cond = jnp.zeros((MOD_ROWS, D_MODEL), F32)
    cond = cond.at[:DEC_BATCH].set(c).at[CTX_MOD_ROW].set(c_ctx)
    mod5 = _modulation(cond, mod_w, mod_b).reshape(DEPTH, MOD_ROWS, N_MOD, 1, D_MODEL)

    vec = lambda a: a.reshape(DEPTH, 1, -1)
    g_mix_pre, g_mix_post = vec(norm_mix_pre), vec(norm_mix_post)
    g_ffn_pre, g_ffn_post = vec(norm_ffn_pre), vec(norm_ffn_post)
    ln_g, ln_b = vec(gmlp_ln_g), vec(gmlp_ln_b)
    gmlp_b4 = gmlp_b.reshape(DEPTH, G_C, CHUNK, 1)
    sink = sink_b.reshape(DEPTH * H_B)
    w_in_b, w_out_b = w_in.astype(BF16), w_out.astype(BF16)
    w_gate_b, w_up_b, w_down_b = w_gate.astype(BF16), w_up.astype(BF16), w_down.astype(BF16)
    bias = _na_bias(rpb_a)
    cos, sin = _rope_tables()

    tm_wide, tm_row = 1024, 512
    ctx_row = lambda i: CTX_MOD_ROW
    lat_row = lambda tm: (lambda i: (i * tm) // DEC_SEQ)

    xp = x_prompt.reshape(N_CTX_TOK, D_MODEL)
    xs = x_sample.reshape(N_LAT_TOK, D_MODEL)
    hp = _norm_mod(xp, mod5, g_mix_pre, 0, ctx_row, tm_row)
    hs = _norm_mod(xs, mod5, g_mix_pre, 0, lat_row(tm_row), tm_row)
    kv_all = None
    for l in range(DEPTH):
        o, *kv_all = _ctx_mix(hp, w_in_b, sink, gmlp_w, gmlp_b4, ln_g, ln_b, kv_all, l)
        xp, hp = _out_proj([o], xp, mod5, g_mix_post, g_ffn_pre, w_out_b, l, ctx_row, tm_row)
        xp, hp = _ffn(hp, xp, mod5, g_ffn_post, g_mix_pre, w_gate_b, w_up_b, w_down_b, l,
                      ctx_row, tm_wide)
        z = _in_proj(hs, w_in_b, l, tm_wide)
        o_a = _lat_na(z, cache_a_k, cache_a_v, bias, l)
        o_b = _lat_win(z, cache_b_k, cache_b_v, sink, cos, sin, l)
        o_c = _lat_gmlp(z, gmlp_w, gmlp_b4, ln_g, ln_b, l)
        xs, hs = _out_proj([o_a, o_b, o_c], xs, mod5, g_mix_post, g_ffn_pre, w_out_b, l,
                           lat_row(tm_row), tm_row)
        xs, hs = _ffn(hs, xs, mod5, g_ffn_post, g_mix_pre, w_gate_b, w_up_b, w_down_b, l,
                      lat_row(tm_wide), tm_wide)

    return (xp.reshape(BATCH, SEQ, D_MODEL), xs.reshape(DEC_BATCH, DEC_SEQ, D_MODEL), *kv_all)
```

```python
import functools

import jax
import jax.numpy as jnp
import numpy as np
from jax import lax
from jax.experimental import pallas as pl
from jax.experimental.pallas import tpu as pltpu

D_MODEL = 2048
BATCH = 32
SEQ = 256
DEPTH = 4
DEC_BATCH = 8
DEC_SEQ = 1024
PAST_LEN = 256
GRID_W = 64
GRID_ROWS = DEC_SEQ // GRID_W
HEAD_DIM = 128
H_A = 6
H_B = 6
KV_B = 2
REP_B = H_B // KV_B
G_C = 4
C_GROUP = 128
W_A = H_A * HEAD_DIM
W_B = H_B * HEAD_DIM
W_C = G_C * C_GROUP
MIX_WIDTH = W_A + W_B + W_C
NA_ROWS = 8
NA_COLS = 16
WIN = 128
BLK = 128
CHUNK = 128
D_FF = 5632
N_MOD = 6
ROPE_BASE = 10000.0
NORM_EPS = 1e-6
ATTN_SCALE = HEAD_DIM ** -0.5
LOG2_E = 1.4426950408889634
SCORE_SCALE = ATTN_SCALE * LOG2_E
Q_A = 0
K_A = Q_A + W_A
V_A = K_A + W_A
Q_B = V_A + W_A
K_B = Q_B + W_B
V_B = K_B + KV_B * HEAD_DIM
U_C = V_B + KV_B * HEAD_DIM
V_C = U_C + W_C
IN_WIDTH = V_C + W_C

N_CTX_TOK = BATCH * SEQ
N_LAT_TOK = DEC_BATCH * DEC_SEQ
MOD_ROWS = 16
CTX_MOD_ROW = DEC_BATCH

VMEM_LIMIT_BYTES = 58 * 1024 * 1024
MXU_TILE = 256

BF16 = jnp.bfloat16
F32 = jnp.float32


def _params(*sem):
    return pltpu.CompilerParams(dimension_semantics=sem, vmem_limit_bytes=VMEM_LIMIT_BYTES)


def _dot(a, b):
    return jnp.dot(a, b, preferred_element_type=F32)


def _dot_nt(a, b):
    return lax.dot_general(a, b, (((1,), (1,)), ((), ())), preferred_element_type=F32)


def _rms(x, g):
    return (x * lax.rsqrt(jnp.mean(x * x, axis=-1, keepdims=True) + NORM_EPS)) * g


def _mod_kernel(c_ref, w_ref, b_ref, o_ref):
    s = jax.nn.silu(c_ref[...])
    o_ref[...] = _dot(s.astype(BF16), w_ref[...].astype(BF16)) + b_ref[...]


def _modulation(cond, mod_w, mod_b):
    tn = 1024
    n = N_MOD * D_MODEL
    return pl.pallas_call(
        _mod_kernel,
        grid=(DEPTH, n // tn),
        in_specs=[
            pl.BlockSpec((MOD_ROWS, D_MODEL), lambda l, j: (0, 0)),
            pl.BlockSpec((None, D_MODEL, tn), lambda l, j: (l, 0, j)),
            pl.BlockSpec((None, 1, tn), lambda l, j: (l, 0, j)),
        ],
        out_specs=pl.BlockSpec((None, MOD_ROWS, tn), lambda l, j: (l, 0, j)),
        out_shape=jax.ShapeDtypeStruct((DEPTH, MOD_ROWS, n), F32),
        compiler_params=_params("parallel", "parallel"),
        name="modulation",
    )(cond, mod_w, mod_b.reshape(DEPTH, 1, n))


def _mod_spec(l, which, row_fn):
    return pl.BlockSpec((None, None, None, 1, D_MODEL),
                        lambda i, *_: (l, row_fn(i), which, 0, 0))


def _vec_spec(l):
    return pl.BlockSpec((None, 1, D_MODEL), lambda i, *_: (l, 0, 0))


def _modulated_norm(x, g, scale, shift):
    return (_rms(x, g) * (1.0 + scale) + shift).astype(BF16)


def _norm_kernel(x_ref, g_ref, sc_ref, sh_ref, h_ref):
    h_ref[...] = _modulated_norm(x_ref[...], g_ref[...], sc_ref[...], sh_ref[...])


def _norm_mod(x, mod5, g_pre, l, row_fn, tm):
    m = x.shape[0]
    return pl.pallas_call(
        _norm_kernel,
        grid=(m // tm,),
        in_specs=[pl.BlockSpec((tm, D_MODEL), lambda i: (i, 0)),
                  _vec_spec(l), _mod_spec(l, 1, row_fn), _mod_spec(l, 0, row_fn)],
        out_specs=pl.BlockSpec((tm, D_MODEL), lambda i: (i, 0)),
        out_shape=jax.ShapeDtypeStruct((m, D_MODEL), BF16),
        compiler_params=_params("parallel"),
        name="norm_mod",
    )(x, g_pre, mod5, mod5)


def _in_kernel(h_ref, w_ref, z_ref):
    z_ref[...] = _dot(h_ref[...], w_ref[...])


def _in_proj(h, w_in, l, tm):
    m = h.shape[0]
    tn = 6 * MXU_TILE
    return pl.pallas_call(
        _in_kernel,
        grid=(IN_WIDTH // tn, m // tm),
        in_specs=[
            pl.BlockSpec((tm, D_MODEL), lambda j, i: (i, 0)),
            pl.BlockSpec((None, D_MODEL, tn), lambda j, i: (l, 0, j)),
        ],
        out_specs=pl.BlockSpec((tm, tn), lambda j, i: (i, j)),
        out_shape=jax.ShapeDtypeStruct((m, IN_WIDTH), F32),
        compiler_params=_params("parallel", "parallel"),
        name="in_proj",
    )(h, w_in)


def _with_ones(v):
    return jnp.concatenate([v.astype(BF16), jnp.ones(v.shape, BF16)], axis=-1)


def _softmax_pv(scores, values, extra_logit=None):
    m = scores[0].max(axis=-1, keepdims=True)
    for s in scores[1:]:
        m = jnp.maximum(m, s.max(axis=-1, keepdims=True))
    if extra_logit is not None:
        m = jnp.maximum(m, extra_logit)
    acc = None
    for s, v in zip(scores, values):
        a = _dot(jnp.exp2(s - m).astype(BF16), v)
        acc = a if acc is None else acc + a
    num, den = acc[:, :HEAD_DIM], acc[:, HEAD_DIM:]
    if extra_logit is not None:
        den = den + jnp.exp2(extra_logit - m)
    return num / den


def _sink_column(sink_ref, base, rows_per_head):
    rows = lax.broadcasted_iota(jnp.int32, (REP_B * rows_per_head, 1), 0)
    col = jnp.full((REP_B * rows_per_head, 1), sink_ref[base] * LOG2_E, F32)
    for r in range(1, REP_B):
        col = jnp.where(rows >= r * rows_per_head, sink_ref[base + r] * LOG2_E, col)
    return col


def _gmlp(u_raw, v_raw, gw_ref, gb_ref, lng_ref, lnb_ref, write):
    u = jax.nn.gelu(u_raw)
    v = jax.nn.gelu(v_raw)
    mu = jnp.mean(v, axis=-1, keepdims=True)
    xc = v - mu
    vn = xc * lax.rsqrt(jnp.mean(xc * xc, axis=-1, keepdims=True) + NORM_EPS)
    vn = (vn * lng_ref[...] + lnb_ref[...]).astype(BF16)
    for n in range(u_raw.shape[0] // CHUNK):
        rs = slice(n * CHUNK, (n + 1) * CHUNK)
        for g in range(G_C):
            cs = slice(g * C_GROUP, (g + 1) * C_GROUP)
            s = _dot(gw_ref[g].astype(BF16), vn[rs, cs]) + gb_ref[g]
            write(n, g, u[rs, cs] * s)


CTX_PROJ_COLS = 2 * MXU_TILE
def _ctx_seqs(l):
    return 1 if l == 0 else 2


def _ctx_mix_kernel(l, sink_ref, h_ref, w_ref, gw_ref, gb_ref, lng_ref, lnb_ref, *refs):
    o_ref, ka_ref, va_ref, kb_ref, vb_ref, z_ref = refs[-6:]
    kv_refs = (ka_ref, va_ref, kb_ref, vb_ref)
    if l == 0:
        for ref in kv_refs:
            ref[:, 1:] = jnp.zeros((ref.shape[0], DEPTH - 1) + ref.shape[2:], F32)
        kv_refs = tuple(ref.at[:, 0] for ref in kv_refs)
    seqs = _ctx_seqs(l)
    for step in _ctx_projection(0, h_ref, w_ref, z_ref):
        step()
    for s in range(seqs):
        ahead = _ctx_projection(s + 1, h_ref, w_ref, z_ref) if s + 1 < seqs else []
        mixers = _ctx_mixers(l, s, sink_ref, gw_ref, gb_ref, lng_ref, lnb_ref, o_ref, *kv_refs, z_ref)
        for i in range(max(len(mixers), len(ahead))):
            if i < len(ahead):
                ahead[i]()
            if i < len(mixers):
                mixers[i]()


def _ctx_projection(s, h_ref, w_ref, z_ref):
    tok = slice(s * SEQ, (s + 1) * SEQ)

    def column_block(c0):
        def run():
            z_ref[tok, c0:c0 + CTX_PROJ_COLS] = _dot(h_ref[tok, :], w_ref[:, c0:c0 + CTX_PROJ_COLS])
        return run

    return [column_block(c0) for c0 in range(0, IN_WIDTH, CTX_PROJ_COLS)]


def _ctx_mixers(l, s, sink_ref, gw_ref, gb_ref, lng_ref, lnb_ref,
                o_ref, ka_ref, va_ref, kb_ref, vb_ref, z_ref):
    tok = slice(s * SEQ, (s + 1) * SEQ)

    def col(off, i=0):
        return z_ref[tok, off + i * HEAD_DIM: off + (i + 1) * HEAD_DIM]

    def head_a(h):
        def run():
            k = col(K_A, h)
            v = col(V_A, h)
            ka_ref[s, h] = k
            va_ref[s, h] = v
            sc = _dot_nt(col(Q_A, h).astype(BF16), k.astype(BF16)) * SCORE_SCALE
            o_ref[tok, h * HEAD_DIM:(h + 1) * HEAD_DIM] = (
                _softmax_pv([sc], [_with_ones(v)]).astype(BF16))
        return run

    def group_b(g):
        def run():
            k = col(K_B, g)
            v = col(V_B, g)
            kb_ref[s, g] = k
            vb_ref[s, g] = v
            q = jnp.concatenate([col(Q_B, g * REP_B + r) for r in range(REP_B)], axis=0).astype(BF16)
            sc = _dot_nt(q, k.astype(BF16)) * SCORE_SCALE
            sink = _sink_column(sink_ref, l * H_B + g * REP_B, SEQ)
            o = _softmax_pv([sc], [_with_ones(v)], sink).astype(BF16)
            for r in range(REP_B):
                c0 = W_A + (g * REP_B + r) * HEAD_DIM
                o_ref[tok, c0:c0 + HEAD_DIM] = o[r * SEQ:(r + 1) * SEQ]
        return run

    def gmlp():
        def write(n, g, val):
            c0 = W_A + W_B + g * C_GROUP
            r0 = s * SEQ + n * CHUNK
            o_ref[r0:r0 + CHUNK, c0:c0 + C_GROUP] = val.astype(BF16)

        _gmlp(z_ref[tok, U_C:V_C], z_ref[tok, V_C:IN_WIDTH], gw_ref, gb_ref, lng_ref, lnb_ref, write)

    return [head_a(h) for h in range(H_A)] + [group_b(g) for g in range(KV_B)] + [gmlp]


def _gmlp_specs(l):
    return [
        pl.BlockSpec((None, G_C, CHUNK, CHUNK), lambda i, *_: (l, 0, 0, 0)),
        pl.BlockSpec((None, G_C, CHUNK, 1), lambda i, *_: (l, 0, 0, 0)),
        pl.BlockSpec((None, 1, W_C), lambda i, *_: (l, 0, 0)),
        pl.BlockSpec((None, 1, W_C), lambda i, *_: (l, 0, 0)),
    ]


def _ctx_mix(h, w_in, sink, gmlp_w, gmlp_b4, ln_g, ln_b, kv_all, l):
    seqs = _ctx_seqs(l)
    rows = seqs * SEQ
    kv_heads = (H_A, H_A, KV_B, KV_B)
    if l == 0:
        kv_spec = lambda nh: pl.BlockSpec((seqs, DEPTH, nh, SEQ, HEAD_DIM), lambda b: (b, 0, 0, 0, 0))
        kv_in, aliases = [], {}
    else:
        kv_spec = lambda nh: pl.BlockSpec((seqs, None, nh, SEQ, HEAD_DIM), lambda b: (b, l, 0, 0, 0))
        kv_in = list(kv_all)
        n_in = 7
        aliases = {n_in + i: 1 + i for i in range(4)}
    return pl.pallas_call(
        functools.partial(_ctx_mix_kernel, l),
        grid=(BATCH // seqs,),
        in_specs=[
            pl.BlockSpec(memory_space=pltpu.SMEM),
            pl.BlockSpec((rows, D_MODEL), lambda b: (b, 0)),
            pl.BlockSpec((None, D_MODEL, IN_WIDTH), lambda b: (l, 0, 0), pipeline_mode=pl.Buffered(1)),
        ] + _gmlp_specs(l) + [pl.BlockSpec(memory_space=pl.ANY)] * len(kv_in),
        out_specs=[pl.BlockSpec((rows, MIX_WIDTH), lambda b: (b, 0))] + [kv_spec(nh) for nh in kv_heads],
        out_shape=[jax.ShapeDtypeStruct((N_CTX_TOK, MIX_WIDTH), BF16)]
                  + [jax.ShapeDtypeStruct((BATCH, DEPTH, nh, SEQ, HEAD_DIM), F32) for nh in kv_heads],
        scratch_shapes=[pltpu.VMEM((rows, IN_WIDTH), F32)],
        input_output_aliases=aliases,
        compiler_params=_params("parallel"),
        name="ctx_mix",
    )(sink, h, w_in, gmlp_w, gmlp_b4, ln_g, ln_b, *kv_in)


NA_QROWS = 4
NA_WROWS = 12
NA_QBLOCKS = GRID_ROWS // NA_QROWS


def _na_band_start(r):
    return min(max(r - NA_ROWS // 2, 0), GRID_ROWS - NA_ROWS)


def _na_window_start(qb):
    return min(max(qb * NA_QROWS - NA_ROWS // 2, 0), GRID_ROWS - NA_WROWS)


def _na_kernel(q_ref, k_ref, v_ref, ck_ref, cv_ref, bias_ref, o_ref):
    ck = ck_ref[...].astype(BF16)
    cv = _with_ones(cv_ref[...])
    v_aug = _with_ones(v_ref[...])

    def window(qb):
        lo = _na_band_start(qb * NA_QROWS) * GRID_W
        hi = (_na_band_start((qb + 1) * NA_QROWS - 1) + NA_ROWS) * GRID_W
        return slice(lo, -(-hi // HEAD_DIM) * HEAD_DIM)

    def bias_lanes(qb):
        first = _na_window_start(qb) * GRID_W
        keys = window(qb)
        return slice(keys.start - first, keys.stop - first)

    def query_rows(qb):
        return slice(qb * NA_QROWS * GRID_W, (qb + 1) * NA_QROWS * GRID_W)

    scores = []
    for qb in range(NA_QBLOCKS):
        q = q_ref[query_rows(qb), :].astype(BF16)
        s_loc = (_dot_nt(q, k_ref[window(qb), :].astype(BF16)) * SCORE_SCALE
                 + bias_ref[qb, :, bias_lanes(qb)])
        s_ctx = _dot_nt(q, ck) * SCORE_SCALE
        scores.append((s_loc, s_ctx))
    for qb in range(NA_QBLOCKS):
        o = _softmax_pv(list(scores[qb]), [v_aug[window(qb), :], cv])
        o_ref[query_rows(qb), :] = o.astype(BF16)


def _na_bias(rpb):
    n_dc = 2 * NA_COLS - 1
    edge = GRID_W - NA_COLS
    ext = jnp.concatenate([jnp.repeat(rpb[..., :1], edge, axis=-1), rpb,
                           jnp.repeat(rpb[..., -1:], edge, axis=-1)], axis=-1).astype(F32)
    period = 2 * GRID_W - 1
    reps = -(-(GRID_W * 2 * GRID_W) // period)
    shifted = jnp.tile(ext, reps)[..., :GRID_W * 2 * GRID_W].reshape(ext.shape[:-1] + (GRID_W, 2 * GRID_W))
    toep = shifted[..., ::-1, :GRID_W]
    col = np.arange(GRID_W)
    c0 = np.clip(col - NA_COLS // 2, 0, GRID_W - NA_COLS)
    col_ok = (col[None, :] >= c0[:, None]) & (col[None, :] < c0[:, None] + NA_COLS)
    toep = jnp.where(col_ok, toep * LOG2_E, -jnp.inf)
    by_q = toep.transpose(0, 1, 3, 2, 4).reshape(DEPTH, H_A, GRID_W, (2 * NA_ROWS - 1) * GRID_W)
    per_row = []
    for r in range(GRID_ROWS):
        w0 = _na_window_start(r // NA_QROWS)
        r0 = _na_band_start(r)
        a0 = r0 - r + NA_ROWS - 1
        band = by_q[..., a0 * GRID_W:(a0 + NA_ROWS) * GRID_W]
        left = (r0 - w0) * GRID_W
        right = (NA_WROWS - NA_ROWS) * GRID_W - left
        per_row.append(jnp.pad(band, ((0, 0), (0, 0), (0, 0), (left, right)),
                               constant_values=-jnp.inf))
    bias = jnp.concatenate(per_row, axis=2)
    return bias.reshape(DEPTH, H_A, NA_QBLOCKS, NA_QROWS * GRID_W, NA_WROWS * GRID_W)


def _lat_na(z, cache_k, cache_v, bias, l):
    zc = lambda off: pl.BlockSpec((DEC_SEQ, HEAD_DIM), lambda h, b: (b, off // HEAD_DIM + h))
    cache = pl.BlockSpec((None, None, None, PAST_LEN, HEAD_DIM), lambda h, b: (b, l, h, 0, 0))
    return pl.pallas_call(
        _na_kernel,
        grid=(H_A, DEC_BATCH),
        in_specs=[zc(Q_A), zc(K_A), zc(V_A), cache, cache,
                  pl.BlockSpec((None, None, NA_QBLOCKS, NA_QROWS * GRID_W, NA_WROWS * GRID_W),
                               lambda h, b: (l, h, 0, 0, 0))],
        out_specs=pl.BlockSpec((DEC_SEQ, HEAD_DIM), lambda h, b: (b, h)),
        out_shape=jax.ShapeDtypeStruct((N_LAT_TOK, W_A), BF16),
        compiler_params=_params("parallel", "parallel"),
        name="lat_na",
    )(z, z, z, cache_k, cache_v, bias)


def _rope(x, cos, sin_signed):
    lane = lax.broadcasted_iota(jnp.int32, x.shape, 1)
    partner = jnp.where(lane % (HEAD_DIM // 2) < HEAD_DIM // 4,
                        pltpu.roll(x, HEAD_DIM - HEAD_DIM // 4, 1),
                        pltpu.roll(x, HEAD_DIM // 4, 1))
    return x * cos + partner * sin_signed


def _rope_tables():
    half = HEAD_DIM // 2
    quarter = half // 2
    t = np.arange(DEC_SEQ)
    freqs = ROPE_BASE ** (-jnp.arange(quarter, dtype=F32) / quarter)

    def tables(pos):
        ang = jnp.asarray(pos, F32)[:, None] * freqs[None, :]
        cos, sin = jnp.cos(ang), jnp.sin(ang)
        return jnp.concatenate([cos, cos], axis=-1), jnp.concatenate([-sin, sin], axis=-1)

    cr, sr = tables(t // GRID_W)
    cc, sc = tables(t % GRID_W)
    return jnp.concatenate([cr, cc], axis=-1), jnp.concatenate([sr, sc], axis=-1)


def _win_kernel(l, sink_ref, q_ref, k_ref, v_ref, ck_ref, cv_ref, cos_ref, sin_ref, o_ref, kr_ref):
    g = pl.program_id(1)
    ck = ck_ref[...].astype(BF16)
    cv = _with_ones(cv_ref[...])
    v_aug = _with_ones(v_ref[...])
    kr_ref[...] = _rope(k_ref[...], cos_ref[...], sin_ref[...]).astype(BF16)
    sink = _sink_column(sink_ref, l * H_B + g * REP_B, BLK)
    n_blk = DEC_SEQ // BLK
    span = BLK + 2 * WIN

    def window_bias(first_key_offset):
        qpos = first_key_offset + lax.broadcasted_iota(jnp.int32, (REP_B * BLK, span), 0) % BLK
        kpos = lax.broadcasted_iota(jnp.int32, (REP_B * BLK, span), 1)
        return jnp.where(jnp.abs(qpos - kpos) <= WIN, 0.0, -jnp.inf).astype(F32)

    bias_first = window_bias(0)
    bias_rest = window_bias(BLK)
    def key_span(nb):
        return slice(max(nb - 1, 0) * BLK, min(nb + 2, n_blk) * BLK)

    def block_scores(nb):
        rows = slice(nb * BLK, (nb + 1) * BLK)
        span_nb = key_span(nb)
        cos = cos_ref[rows, :]
        sin = sin_ref[rows, :]
        q = jnp.concatenate(
            [_rope(q_ref[rows, r * HEAD_DIM:(r + 1) * HEAD_DIM], cos, sin) for r in range(REP_B)],
            axis=0).astype(BF16)
        bias = (bias_first if nb == 0 else bias_rest)[:, :span_nb.stop - span_nb.start]
        s_loc = _dot_nt(q, kr_ref[span_nb, :]) * SCORE_SCALE + bias
        s_ctx = _dot_nt(q, ck) * SCORE_SCALE
        return s_loc, s_ctx

    def block_output(nb, scores):
        rows = slice(nb * BLK, (nb + 1) * BLK)
        o = _softmax_pv(list(scores), [v_aug[key_span(nb), :], cv], sink).astype(BF16)
        for r in range(REP_B):
            o_ref[rows, r * HEAD_DIM:(r + 1) * HEAD_DIM] = o[r * BLK:(r + 1) * BLK]

    scores = [block_scores(nb) for nb in range(n_blk)]
    for nb in range(n_blk):
        block_output(nb, scores[nb])


def _lat_win(z, cache_k, cache_v, sink, cos, sin, l):
    zc = lambda off: pl.BlockSpec((DEC_SEQ, HEAD_DIM), lambda b, g: (b, off // HEAD_DIM + g))
    cache = pl.BlockSpec((None, None, None, PAST_LEN, HEAD_DIM), lambda b, g: (b, l, g, 0, 0))
    tab = pl.BlockSpec((DEC_SEQ, HEAD_DIM), lambda b, g: (0, 0))
    qw = REP_B * HEAD_DIM
    return pl.pallas_call(
        functools.partial(_win_kernel, l),
        grid=(DEC_BATCH, KV_B),
        in_specs=[pl.BlockSpec(memory_space=pltpu.SMEM),
                  pl.BlockSpec((DEC_SEQ, qw), lambda b, g: (b, Q_B // qw + g)),
                  zc(K_B), zc(V_B), cache, cache, tab, tab],
        out_specs=pl.BlockSpec((DEC_SEQ, qw), lambda b, g: (b, g)),
        out_shape=jax.ShapeDtypeStruct((N_LAT_TOK, W_B), BF16),
        scratch_shapes=[pltpu.VMEM((DEC_SEQ, HEAD_DIM), BF16)],
        compiler_params=_params("parallel", "parallel"),
        name="lat_win",
    )(sink, z, z, z, cache_k, cache_v, cos, sin)


def _gmlp_kernel(u_ref, v_ref, gw_ref, gb_ref, lng_ref, lnb_ref, o_ref):
    def write(n, g, val):
        o_ref[n * CHUNK:(n + 1) * CHUNK, g * C_GROUP:(g + 1) * C_GROUP] = val.astype(BF16)

    _gmlp(u_ref[...], v_ref[...], gw_ref, gb_ref, lng_ref, lnb_ref, write)


def _lat_gmlp(z, gmlp_w, gmlp_b4, ln_g, ln_b, l):
    rows = DEC_SEQ
    return pl.pallas_call(
        _gmlp_kernel,
        grid=(N_LAT_TOK // rows,),
        in_specs=[pl.BlockSpec((rows, W_C), lambda i: (i, U_C // W_C)),
                  pl.BlockSpec((rows, W_C), lambda i: (i, V_C // W_C))] + _gmlp_specs(l),
        out_specs=pl.BlockSpec((rows, W_C), lambda i: (i, 0)),
        out_shape=jax.ShapeDtypeStruct((N_LAT_TOK, W_C), BF16),
        compiler_params=_params("parallel"),
        name="lat_gmlp",
    )(z, z, gmlp_w, gmlp_b4, ln_g, ln_b)


EPI_ROWS = 128


def _residual_epilogue(y, rows, x_ref, g_post_ref, gate_ref, xo_ref, next_norm):
    x_new = x_ref[rows, :] + _rms(y, gate_ref[...] * g_post_ref[...])
    xo_ref[rows, :] = x_new
    if next_norm is not None:
        g_ref, sc_ref, sh_ref, h_ref = next_norm
        h_ref[rows, :] = (_rms(x_new, g_ref[...] * (1.0 + sc_ref[...])) + sh_ref[...]).astype(BF16)


def _out_kernel(widths, *refs):
    o_refs = refs[:len(widths)]
    (w_ref, x_ref, g_post_ref, gate_ref, g_next_ref, sc_ref, sh_ref,
     xo_ref, h_ref) = refs[len(widths):]
    for r in range(x_ref.shape[0] // EPI_ROWS):
        rows = slice(r * EPI_ROWS, (r + 1) * EPI_ROWS)
        y = None
        off = 0
        for o_ref, wd in zip(o_refs, widths):
            part = _dot(o_ref[rows, :], w_ref[off:off + wd, :])
            y = part if y is None else y + part
            off += wd
        _residual_epilogue(y, rows, x_ref, g_post_ref, gate_ref, xo_ref,
                           (g_next_ref, sc_ref, sh_ref, h_ref))


def _out_proj(o_list, x, mod5, g_post, g_ffn_pre, w_out, l, row_fn, tm):
    m = x.shape[0]
    widths = tuple(o.shape[1] for o in o_list)
    row_spec = pl.BlockSpec((tm, D_MODEL), lambda i: (i, 0))
    return pl.pallas_call(
        functools.partial(_out_kernel, widths),
        grid=(m // tm,),
        in_specs=[pl.BlockSpec((tm, wd), lambda i: (i, 0)) for wd in widths] + [
            pl.BlockSpec((None, MIX_WIDTH, D_MODEL), lambda i: (l, 0, 0)),
            row_spec,
            _vec_spec(l),
            _mod_spec(l, 2, row_fn),
            _vec_spec(l),
            _mod_spec(l, 4, row_fn),
            _mod_spec(l, 3, row_fn),
        ],
        out_specs=[row_spec, row_spec],
        out_shape=[jax.ShapeDtypeStruct((m, D_MODEL), F32),
                   jax.ShapeDtypeStruct((m, D_MODEL), BF16)],
        compiler_params=_params("parallel"),
        name="out_proj",
    )(*o_list, w_out, x, g_post, mod5, g_ffn_pre, mod5, mod5)


FFN_TC = 2 * MXU_TILE
FFN_ROWS = 512
FFN_LAST_ROWS = 256


def _ffn_kernel(with_next, h_ref, wg_ref, wu_ref, wd_ref, x_hbm, g_post_ref, gate_ref, *refs):
    if with_next:
        g_next_ref, sc_ref, sh_ref, xo_ref, ho_ref, x_buf, x_sem = refs
        next_norm = (g_next_ref, sc_ref, sh_ref, ho_ref)
    else:
        xo_ref, x_buf, x_sem = refs
        next_norm = None
    tm = x_buf.shape[0]
    c = pl.program_id(1)
    last = pl.num_programs(1) - 1

    def x_copy():
        row0 = pl.multiple_of(pl.program_id(0) * tm, tm)
        return pltpu.make_async_copy(x_hbm.at[pl.ds(row0, tm), :], x_buf, x_sem)

    def row_blocks(n):
        return [slice(r * n, (r + 1) * n) for r in range(tm // n)]

    def chunk_products(n):
        acts = []
        for rows in row_blocks(n):
            h = h_ref[rows, :]
            acts.append((jax.nn.silu(_dot(h, wg_ref[...])) * _dot(h, wu_ref[...])).astype(BF16))
        for rows, a in zip(row_blocks(n), acts):
            yield rows, _dot(a, wd_ref[...])

    @pl.when(c == 0)
    def _():
        x_copy().start()
        for rows, y in chunk_products(FFN_ROWS):
            xo_ref[rows, :] = y

    @pl.when((c > 0) & (c < last))
    def _():
        for rows, y in chunk_products(FFN_ROWS):
            xo_ref[rows, :] += y

    @pl.when(c == last)
    def _():
        x_copy().wait()
        for rows, y in chunk_products(FFN_LAST_ROWS):
            _residual_epilogue(xo_ref[rows, :] + y, rows, x_buf, g_post_ref, gate_ref, xo_ref,
                               next_norm)


def _ffn(h, x, mod5, g_post, g_mix_pre, w_gate, w_up, w_down, l, row_fn, tm):
    m = x.shape[0]
    with_next = l + 1 < DEPTH
    row_spec = pl.BlockSpec((tm, D_MODEL), lambda i, c: (i, 0))
    up_spec = pl.BlockSpec((None, D_MODEL, FFN_TC), lambda i, c: (l, 0, c))
    in_specs = [
        row_spec, up_spec, up_spec,
        pl.BlockSpec((None, FFN_TC, D_MODEL), lambda i, c: (l, c, 0)),
        pl.BlockSpec(memory_space=pl.ANY),
        _vec_spec(l),
        _mod_spec(l, 5, row_fn),
    ]
    args = [h, w_gate, w_up, w_down, x, g_post, mod5]
    out_specs = [row_spec]
    out_shape = [jax.ShapeDtypeStruct((m, D_MODEL), F32)]
    if with_next:
        in_specs += [_vec_spec(l + 1), _mod_spec(l + 1, 1, row_fn), _mod_spec(l + 1, 0, row_fn)]
        args += [g_mix_pre, mod5, mod5]
        out_specs.append(row_spec)
        out_shape.append(jax.ShapeDtypeStruct((m, D_MODEL), BF16))
    outs = pl.pallas_call(
        functools.partial(_ffn_kernel, with_next),
        grid=(m // tm, D_FF // FFN_TC),
        in_specs=in_specs,
        out_specs=out_specs,
        out_shape=out_shape,
        scratch_shapes=[pltpu.VMEM((tm, D_MODEL), F32), pltpu.SemaphoreType.DMA(())],
        compiler_params=_params("parallel", "arbitrary"),
        name="ffn",
    )(*args)
    return outs if with_next else (outs[0], None)


def kernel(x_prompt, x_sample, cache_a_k, cache_a_v, cache_b_k, cache_b_v, c, c_ctx,
           mod_w, mod_b, norm_mix_pre, norm_mix_post, norm_ffn_pre, norm_ffn_post,
           w_in, w_out, rpb_a, sink_b, gmlp_ln_g, gmlp_ln_b, gmlp_w, gmlp_b,
           w_gate, w_up, w_down):
    cond = jnp.zeros((MOD_ROWS, D_MODEL), F32)
    cond = cond.at[:DEC_BATCH].set(c).at[CTX_MOD_ROW].set(c_ctx)
    mod5 = _modulation(cond, mod_w, mod_b).reshape(DEPTH, MOD_ROWS, N_MOD, 1, D_MODEL)

    vec = lambda a: a.reshape(DEPTH, 1, -1)
    g_mix_pre, g_mix_post = vec(norm_mix_pre), vec(norm_mix_post)
    g_ffn_pre, g_ffn_post = vec(norm_ffn_pre), vec(norm_ffn_post)
    ln_g, ln_b = vec(gmlp_ln_g), vec(gmlp_ln_b)
    gmlp_b4 = gmlp_b.reshape(DEPTH, G_C, CHUNK, 1)
    sink = sink_b.reshape(DEPTH * H_B)
    w_in_b, w_out_b = w_in.astype(BF16), w_out.astype(BF16)
    w_gate_b, w_up_b, w_down_b = w_gate.astype(BF16), w_up.astype(BF16), w_down.astype(BF16)
    bias = _na_bias(rpb_a)
    cos, sin = _rope_tables()

    tm_wide, tm_row = 1024, 512
    ctx_row = lambda i: CTX_MOD_ROW
    lat_row = lambda tm: (lambda i: (i * tm) // DEC_SEQ)

    xp = x_prompt.reshape(N_CTX_TOK, D_MODEL)
    xs = x_sample.reshape(N_LAT_TOK, D_MODEL)
    hp = _norm_mod(xp, mod5, g_mix_pre, 0, ctx_row, tm_row)
    hs = _norm_mod(xs, mod5, g_mix_pre, 0, lat_row(tm_row), tm_row)
    kv_all = None
    for l in range(DEPTH):
        o, *kv_all = _ctx_mix(hp, w_in_b, sink, gmlp_w, gmlp_b4, ln_g, ln_b, kv_all, l)
        xp, hp = _out_proj([o], xp, mod5, g_mix_post, g_ffn_pre, w_out_b, l, ctx_row, tm_row)
        xp, hp = _ffn(hp, xp, mod5, g_ffn_post, g_mix_pre, w_gate_b, w_up_b, w_down_b, l,
                      ctx_row, tm_wide)
        z = _in_proj(hs, w_in_b, l, tm_wide)
        o_a = _lat_na(z, cache_a_k, cache_a_v, bias, l)
        o_b = _lat_win(z, cache_b_k, cache_b_v, sink, cos, sin, l)
        o_c = _lat_gmlp(z, gmlp_w, gmlp_b4, ln_g, ln_b, l)
        xs, hs = _out_proj([o_a, o_b, o_c], xs, mod5, g_mix_post, g_ffn_pre, w_out_b, l,
                           lat_row(tm_row), tm_row)
        xs, hs = _ffn(hs, xs, mod5, g_ffn_post, g_mix_pre, w_gate_b, w_up_b, w_down_b, l,
                      lat_row(tm_wide), tm_wide)

    return (xp.reshape(BATCH, SEQ, D_MODEL), xs.reshape(DEC_BATCH, DEC_SEQ, D_MODEL), *kv_all)
```

```python
import functools

import jax
import jax.numpy as jnp
import numpy as np
from jax import lax
from jax.experimental import pallas as pl
from jax.experimental.pallas import tpu as pltpu

D_MODEL = 2048
BATCH = 32
SEQ = 256
DEPTH = 4
DEC_BATCH = 8
DEC_SEQ = 1024
PAST_LEN = 256
GRID_W = 64
GRID_ROWS = DEC_SEQ // GRID_W
HEAD_DIM = 128
H_A = 6
H_B = 6
KV_B = 2
REP_B = H_B // KV_B
G_C = 4
C_GROUP = 128
W_A = H_A * HEAD_DIM
W_B = H_B * HEAD_DIM
W_C = G_C * C_GROUP
MIX_WIDTH = W_A + W_B + W_C
NA_ROWS = 8
NA_COLS = 16
WIN = 128
BLK = 128
CHUNK = 128
D_FF = 5632
N_MOD = 6
ROPE_BASE = 10000.0
NORM_EPS = 1e-6
ATTN_SCALE = HEAD_DIM ** -0.5
LOG2_E = 1.4426950408889634
SCORE_SCALE = ATTN_SCALE * LOG2_E
Q_A = 0
K_A = Q_A + W_A
V_A = K_A + W_A
Q_B = V_A + W_A
K_B = Q_B + W_B
V_B = K_B + KV_B * HEAD_DIM
U_C = V_B + KV_B * HEAD_DIM
V_C = U_C + W_C
IN_WIDTH = V_C + W_C

N_CTX_TOK = BATCH * SEQ
N_LAT_TOK = DEC_BATCH * DEC_SEQ
MOD_ROWS = 16
CTX_MOD_ROW = DEC_BATCH

VMEM_LIMIT_BYTES = 58 * 1024 * 1024
MXU_TILE = 256

BF16 = jnp.bfloat16
F32 = jnp.float32


def _params(*sem):
    return pltpu.CompilerParams(dimension_semantics=sem, vmem_limit_bytes=VMEM_LIMIT_BYTES)


def _dot(a, b):
    return jnp.dot(a, b, preferred_element_type=F32)


def _dot_nt(a, b):
    return lax.dot_general(a, b, (((1,), (1,)), ((), ())), preferred_element_type=F32)


def _rms(x, g):
    return (x * lax.rsqrt(jnp.mean(x * x, axis=-1, keepdims=True) + NORM_EPS)) * g


def _mod_kernel(c_ref, w_ref, b_ref, o_ref):
    s = jax.nn.silu(c_ref[...])
    o_ref[...] = _dot(s.astype(BF16), w_ref[...].astype(BF16)) + b_ref[...]


def _modulation(cond, mod_w, mod_b):
    tn = 1024
    n = N_MOD * D_MODEL
    return pl.pallas_call(
        _mod_kernel,
        grid=(DEPTH, n // tn),
        in_specs=[
            pl.BlockSpec((MOD_ROWS, D_MODEL), lambda l, j: (0, 0)),
            pl.BlockSpec((None, D_MODEL, tn), lambda l, j: (l, 0, j)),
            pl.BlockSpec((None, 1, tn), lambda l, j: (l, 0, j)),
        ],
        out_specs=pl.BlockSpec((None, MOD_ROWS, tn), lambda l, j: (l, 0, j)),
        out_shape=jax.ShapeDtypeStruct((DEPTH, MOD_ROWS, n), F32),
        compiler_params=_params("parallel", "parallel"),
        name="modulation",
    )(cond, mod_w, mod_b.reshape(DEPTH, 1, n))


def _mod_spec(l, which, row_fn):
    return pl.BlockSpec((None, None, None, 1, D_MODEL),
                        lambda i, *_: (l, row_fn(i), which, 0, 0))


def _vec_spec(l):
    return pl.BlockSpec((None, 1, D_MODEL), lambda i, *_: (l, 0, 0))


def _modulated_norm(x, g, scale, shift):
    return (_rms(x, g) * (1.0 + scale) + shift).astype(BF16)


def _norm_kernel(x_ref, g_ref, sc_ref, sh_ref, h_ref):
    h_ref[...] = _modulated_norm(x_ref[...], g_ref[...], sc_ref[...], sh_ref[...])


def _norm_mod(x, mod5, g_pre, l, row_fn, tm):
    m = x.shape[0]
    return pl.pallas_call(
        _norm_kernel,
        grid=(m // tm,),
        in_specs=[pl.BlockSpec((tm, D_MODEL), lambda i: (i, 0)),
                  _vec_spec(l), _mod_spec(l, 1, row_fn), _mod_spec(l, 0, row_fn)],
        out_specs=pl.BlockSpec((tm, D_MODEL), lambda i: (i, 0)),
        out_shape=jax.ShapeDtypeStruct((m, D_MODEL), BF16),
        compiler_params=_params("parallel"),
        name="norm_mod",
    )(x, g_pre, mod5, mod5)


def _in_kernel(h_ref, w_ref, z_ref):
    z_ref[...] = _dot(h_ref[...], w_ref[...])


def _in_proj(h, w_in, l, tm):
    m = h.shape[0]
    tn = 6 * MXU_TILE
    return pl.pallas_call(
        _in_kernel,
        grid=(IN_WIDTH // tn, m // tm),
        in_specs=[
            pl.BlockSpec((tm, D_MODEL), lambda j, i: (i, 0)),
            pl.BlockSpec((None, D_MODEL, tn), lambda j, i: (l, 0, j)),
        ],
        out_specs=pl.BlockSpec((tm, tn), lambda j, i: (i, j)),
        out_shape=jax.ShapeDtypeStruct((m, IN_WIDTH), F32),
        compiler_params=_params("parallel", "parallel"),
        name="in_proj",
    )(h, w_in)


def _with_ones(v):
    return jnp.concatenate([v.astype(BF16), jnp.ones(v.shape, BF16)], axis=-1)


def _softmax_pv(scores, values, extra_logit=None):
    m = scores[0].max(axis=-1, keepdims=True)
    for s in scores[1:]:
        m = jnp.maximum(m, s.max(axis=-1, keepdims=True))
    if extra_logit is not None:
        m = jnp.maximum(m, extra_logit)
    acc = None
    for s, v in zip(scores, values):
        a = _dot(jnp.exp2(s - m).astype(BF16), v)
        acc = a if acc is None else acc + a
    num, den = acc[:, :HEAD_DIM], acc[:, HEAD_DIM:]
    if extra_logit is not None:
        den = den + jnp.exp2(extra_logit - m)
    return num / den


def _sink_column(sink_ref, base, rows_per_head):
    rows = lax.broadcasted_iota(jnp.int32, (REP_B * rows_per_head, 1), 0)
    col = jnp.full((REP_B * rows_per_head, 1), sink_ref[base] * LOG2_E, F32)
    for r in range(1, REP_B):
        col = jnp.where(rows >= r * rows_per_head, sink_ref[base + r] * LOG2_E, col)
    return col


def _gmlp(u_raw, v_raw, gw_ref, gb_ref, lng_ref, lnb_ref, write):
    u = jax.nn.gelu(u_raw)
    v = jax.nn.gelu(v_raw)
    mu = jnp.mean(v, axis=-1, keepdims=True)
    xc = v - mu
    vn = xc * lax.rsqrt(jnp.mean(xc * xc, axis=-1, keepdims=True) + NORM_EPS)
    vn = (vn * lng_ref[...] + lnb_ref[...]).astype(BF16)
    for n in range(u_raw.shape[0] // CHUNK):
        rs = slice(n * CHUNK, (n + 1) * CHUNK)
        for g in range(G_C):
            cs = slice(g * C_GROUP, (g + 1) * C_GROUP)
            s = _dot(gw_ref[g].astype(BF16), vn[rs, cs]) + gb_ref[g]
            write(n, g, u[rs, cs] * s)


CTX_PROJ_COLS = 2 * MXU_TILE
def _ctx_seqs(l):
    return 1 if l == 0 else 2


def _ctx_mix_kernel(l, sink_ref, h_ref, w_ref, gw_ref, gb_ref, lng_ref, lnb_ref, *refs):
    o_ref, ka_ref, va_ref, kb_ref, vb_ref, z_ref = refs[-6:]
    kv_refs = (ka_ref, va_ref, kb_ref, vb_ref)
    if l == 0:
        for ref in kv_refs:
            ref[:, 1:] = jnp.zeros((ref.shape[0], DEPTH - 1) + ref.shape[2:], F32)
        kv_refs = tuple(ref.at[:, 0] for ref in kv_refs)
    seqs = _ctx_seqs(l)
    for step in _ctx_projection(0, h_ref, w_ref, z_ref):
        step()
    for s in range(seqs):
        ahead = _ctx_projection(s + 1, h_ref, w_ref, z_ref) if s + 1 < seqs else []
        mixers = _ctx_mixers(l, s, sink_ref, gw_ref, gb_ref, lng_ref, lnb_ref, o_ref, *kv_refs, z_ref)
        for i in range(max(len(mixers), len(ahead))):
            if i < len(ahead):
                ahead[i]()
            if i < len(mixers):
                mixers[i]()


def _ctx_projection(s, h_ref, w_ref, z_ref):
    tok = slice(s * SEQ, (s + 1) * SEQ)

    def column_block(c0):
        def run():
            z_ref[tok, c0:c0 + CTX_PROJ_COLS] = _dot(h_ref[tok, :], w_ref[:, c0:c0 + CTX_PROJ_COLS])
        return run

    return [column_block(c0) for c0 in range(0, IN_WIDTH, CTX_PROJ_COLS)]


def _ctx_mixers(l, s, sink_ref, gw_ref, gb_ref, lng_ref, lnb_ref,
                o_ref, ka_ref, va_ref, kb_ref, vb_ref, z_ref):
    tok = slice(s * SEQ, (s + 1) * SEQ)

    def col(off, i=0):
        return z_ref[tok, off + i * HEAD_DIM: off + (i + 1) * HEAD_DIM]

    def head_a(h):
        def run():
            k = col(K_A, h)
            v = col(V_A, h)
            ka_ref[s, h] = k
            va_ref[s, h] = v
            sc = _dot_nt(col(Q_A, h).astype(BF16), k.astype(BF16)) * SCORE_SCALE
            o_ref[tok, h * HEAD_DIM:(h + 1) * HEAD_DIM] = (
                _softmax_pv([sc], [_with_ones(v)]).astype(BF16))
        return run

    def group_b(g):
        def run():
            k = col(K_B, g)
            v = col(V_B, g)
            kb_ref[s, g] = k
            vb_ref[s, g] = v
            q = jnp.concatenate([col(Q_B, g * REP_B + r) for r in range(REP_B)], axis=0).astype(BF16)
            sc = _dot_nt(q, k.astype(BF16)) * SCORE_SCALE
            sink = _sink_column(sink_ref, l * H_B + g * REP_B, SEQ)
            o = _softmax_pv([sc], [_with_ones(v)], sink).astype(BF16)
            for r in range(REP_B):
                c0 = W_A + (g * REP_B + r) * HEAD_DIM
                o_ref[tok, c0:c0 + HEAD_DIM] = o[r * SEQ:(r + 1) * SEQ]
        return run

    def gmlp():
        def write(n, g, val):
            c0 = W_A + W_B + g * C_GROUP
            r0 = s * SEQ + n * CHUNK
            o_ref[r0:r0 + CHUNK, c0:c0 + C_GROUP] = val.astype(BF16)

        _gmlp(z_ref[tok, U_C:V_C], z_ref[tok, V_C:IN_WIDTH], gw_ref, gb_ref, lng_ref, lnb_ref, write)

    return [head_a(h) for h in range(H_A)] + [group_b(g) for g in range(KV_B)] + [gmlp]


def _gmlp_specs(l):
    return [
        pl.BlockSpec((None, G_C, CHUNK, CHUNK), lambda i, *_: (l, 0, 0, 0)),
        pl.BlockSpec((None, G_C, CHUNK, 1), lambda i, *_: (l, 0, 0, 0)),
        pl.BlockSpec((None, 1, W_C), lambda i, *_: (l, 0, 0)),
        pl.BlockSpec((None, 1, W_C), lambda i, *_: (l, 0, 0)),
    ]


def _ctx_mix(h, w_in, sink, gmlp_w, gmlp_b4, ln_g, ln_b, kv_all, l):
    seqs = _ctx_seqs(l)
    rows = seqs * SEQ
    kv_heads = (H_A, H_A, KV_B, KV_B)
    if l == 0:
        kv_spec = lambda nh: pl.BlockSpec((seqs, DEPTH, nh, SEQ, HEAD_DIM), lambda b: (b, 0, 0, 0, 0))
        kv_in, aliases = [], {}
    else:
        kv_spec = lambda nh: pl.BlockSpec((seqs, None, nh, SEQ, HEAD_DIM), lambda b: (b, l, 0, 0, 0))
        kv_in = list(kv_all)
        n_in = 7
        aliases = {n_in + i: 1 + i for i in range(4)}
    return pl.pallas_call(
        functools.partial(_ctx_mix_kernel, l),
        grid=(BATCH // seqs,),
        in_specs=[
            pl.BlockSpec(memory_space=pltpu.SMEM),
            pl.BlockSpec((rows, D_MODEL), lambda b: (b, 0)),
            pl.BlockSpec((None, D_MODEL, IN_WIDTH), lambda b: (l, 0, 0), pipeline_mode=pl.Buffered(1)),
        ] + _gmlp_specs(l) + [pl.BlockSpec(memory_space=pl.ANY)] * len(kv_in),
        out_specs=[pl.BlockSpec((rows, MIX_WIDTH), lambda b: (b, 0))] + [kv_spec(nh) for nh in kv_heads],
        out_shape=[jax.ShapeDtypeStruct((N_CTX_TOK, MIX_WIDTH), BF16)]
                  + [jax.ShapeDtypeStruct((BATCH, DEPTH, nh, SEQ, HEAD_DIM), F32) for nh in kv_heads],
        scratch_shapes=[pltpu.VMEM((rows, IN_WIDTH), F32)],
        input_output_aliases=aliases,
        compiler_params=_params("parallel"),
        name="ctx_mix",
    )(sink, h, w_in, gmlp_w, gmlp_b4, ln_g, ln_b, *kv_in)


NA_QROWS = 4
NA_WROWS = 12
NA_QBLOCKS = GRID_ROWS // NA_QROWS


def _na_band_start(r):
    return min(max(r - NA_ROWS // 2, 0), GRID_ROWS - NA_ROWS)


def _na_window_start(qb):
    return min(max(qb * NA_QROWS - NA_ROWS // 2, 0), GRID_ROWS - NA_WROWS)


def _na_kernel(q_ref, k_ref, v_ref, ck_ref, cv_ref, bias_ref, o_ref):
    ck = ck_ref[...].astype(BF16)
    cv = _with_ones(cv_ref[...])
    v_aug = _with_ones(v_ref[...])

    def window(qb):
        lo = _na_band_start(qb * NA_QROWS) * GRID_W
        hi = (_na_band_start((qb + 1) * NA_QROWS - 1) + NA_ROWS) * GRID_W
        return slice(lo, -(-hi // HEAD_DIM) * HEAD_DIM)

    def bias_lanes(qb):
        first = _na_window_start(qb) * GRID_W
        keys = window(qb)
        return slice(keys.start - first, keys.stop - first)

    def query_rows(qb):
        return slice(qb * NA_QROWS * GRID_W, (qb + 1) * NA_QROWS * GRID_W)

    scores = []
    for qb in range(NA_QBLOCKS):
        q = q_ref[query_rows(qb), :].astype(BF16)
        s_loc = (_dot_nt(q, k_ref[window(qb), :].astype(BF16)) * SCORE_SCALE
                 + bias_ref[qb, :, bias_lanes(qb)])
        s_ctx = _dot_nt(q, ck) * SCORE_SCALE
        scores.append((s_loc, s_ctx))
    for qb in range(NA_QBLOCKS):
        o = _softmax_pv(list(scores[qb]), [v_aug[window(qb), :], cv])
        o_ref[query_rows(qb), :] = o.astype(BF16)


def _na_bias(rpb):
    n_dc = 2 * NA_COLS - 1
    edge = GRID_W - NA_COLS
    ext = jnp.concatenate([jnp.repeat(rpb[..., :1], edge, axis=-1), rpb,
                           jnp.repeat(rpb[..., -1:], edge, axis=-1)], axis=-1).astype(F32)
    period = 2 * GRID_W - 1
    reps = -(-(GRID_W * 2 * GRID_W) // period)
    shifted = jnp.tile(ext, reps)[..., :GRID_W * 2 * GRID_W].reshape(ext.shape[:-1] + (GRID_W, 2 * GRID_W))
    toep = shifted[..., ::-1, :GRID_W]
    col = np.arange(GRID_W)
    c0 = np.clip(col - NA_COLS // 2, 0, GRID_W - NA_COLS)
    col_ok = (col[None, :] >= c0[:, None]) & (col[None, :] < c0[:, None] + NA_COLS)
    toep = jnp.where(col_ok, toep * LOG2_E, -jnp.inf)
    by_q = toep.transpose(0, 1, 3, 2, 4).reshape(DEPTH, H_A, GRID_W, (2 * NA_ROWS - 1) * GRID_W)
    per_row = []
    for r in range(GRID_ROWS):
        w0 = _na_window_start(r // NA_QROWS)
        r0 = _na_band_start(r)
        a0 = r0 - r + NA_ROWS - 1
        band = by_q[..., a0 * GRID_W:(a0 + NA_ROWS) * GRID_W]
        left = (r0 - w0) * GRID_W
        right = (NA_WROWS - NA_ROWS) * GRID_W - left
        per_row.append(jnp.pad(band, ((0, 0), (0, 0), (0, 0), (left, right)),
                               constant_values=-jnp.inf))
    bias = jnp.concatenate(per_row, axis=2)
    return bias.reshape(DEPTH, H_A, NA_QBLOCKS, NA_QROWS * GRID_W, NA_WROWS * GRID_W)


def _lat_na(z, cache_k, cache_v, bias, l):
    zc = lambda off: pl.BlockSpec((DEC_SEQ, HEAD_DIM), lambda h, b: (b, off // HEAD_DIM + h))
    cache = pl.BlockSpec((None, None, None, PAST_LEN, HEAD_DIM), lambda h, b: (b, l, h, 0, 0))
    return pl.pallas_call(
        _na_kernel,
        grid=(H_A, DEC_BATCH),
        in_specs=[zc(Q_A), zc(K_A), zc(V_A), cache, cache,
                  pl.BlockSpec((None, None, NA_QBLOCKS, NA_QROWS * GRID_W, NA_WROWS * GRID_W),
                               lambda h, b: (l, h, 0, 0, 0))],
        out_specs=pl.BlockSpec((DEC_SEQ, HEAD_DIM), lambda h, b: (b, h)),
        out_shape=jax.ShapeDtypeStruct((N_LAT_TOK, W_A), BF16),
        compiler_params=_params("parallel", "parallel"),
        name="lat_na",
    )(z, z, z, cache_k, cache_v, bias)


def _rope(x, cos, sin_signed):
    lane = lax.broadcasted_iota(jnp.int32, x.shape, 1)
    partner = jnp.where(lane % (HEAD_DIM // 2) < HEAD_DIM // 4,
                        pltpu.roll(x, HEAD_DIM - HEAD_DIM // 4, 1),
                        pltpu.roll(x, HEAD_DIM // 4, 1))
    return x * cos + partner * sin_signed


def _rope_tables():
    half = HEAD_DIM // 2
    quarter = half // 2
    t = np.arange(DEC_SEQ)
    freqs = ROPE_BASE ** (-jnp.arange(quarter, dtype=F32) / quarter)

    def tables(pos):
        ang = jnp.asarray(pos, F32)[:, None] * freqs[None, :]
        cos, sin = jnp.cos(ang), jnp.sin(ang)
        return jnp.concatenate([cos, cos], axis=-1), jnp.concatenate([-sin, sin], axis=-1)

    cr, sr = tables(t // GRID_W)
    cc, sc = tables(t % GRID_W)
    return jnp.concatenate([cr, cc], axis=-1), jnp.concatenate([sr, sc], axis=-1)


def _win_kernel(l, sink_ref, q_ref, k_ref, v_ref, ck_ref, cv_ref, cos_ref, sin_ref,
                u_ref, vc_ref, gw_ref, gb_ref, lng_ref, lnb_ref, o_ref, oc_ref, kr_ref):
    g = pl.program_id(1)

    @pl.when(g == 0)
    def _():
        def write(n, grp, val):
            oc_ref[n * CHUNK:(n + 1) * CHUNK, grp * C_GROUP:(grp + 1) * C_GROUP] = val.astype(BF16)

        _gmlp(u_ref[...], vc_ref[...], gw_ref, gb_ref, lng_ref, lnb_ref, write)

    ck = ck_ref[...].astype(BF16)
    cv = _with_ones(cv_ref[...])
    v_aug = _with_ones(v_ref[...])
    kr_ref[...] = _rope(k_ref[...], cos_ref[...], sin_ref[...]).astype(BF16)
    sink = _sink_column(sink_ref, l * H_B + g * REP_B, BLK)
    n_blk = DEC_SEQ // BLK
    span = BLK + 2 * WIN

    def window_bias(first_key_offset):
        qpos = first_key_offset + lax.broadcasted_iota(jnp.int32, (REP_B * BLK, span), 0) % BLK
        kpos = lax.broadcasted_iota(jnp.int32, (REP_B * BLK, span), 1)
        return jnp.where(jnp.abs(qpos - kpos) <= WIN, 0.0, -jnp.inf).astype(F32)

    bias_first = window_bias(0)
    bias_rest = window_bias(BLK)
    def key_span(nb):
        return slice(max(nb - 1, 0) * BLK, min(nb + 2, n_blk) * BLK)

    def block_scores(nb):
        rows = slice(nb * BLK, (nb + 1) * BLK)
        span_nb = key_span(nb)
        cos = cos_ref[rows, :]
        sin = sin_ref[rows, :]
        q = jnp.concatenate(
            [_rope(q_ref[rows, r * HEAD_DIM:(r + 1) * HEAD_DIM], cos, sin) for r in range(REP_B)],
            axis=0).astype(BF16)
        bias = (bias_first if nb == 0 else bias_rest)[:, :span_nb.stop - span_nb.start]
        s_loc = _dot_nt(q, kr_ref[span_nb, :]) * SCORE_SCALE + bias
        s_ctx = _dot_nt(q, ck) * SCORE_SCALE
        return s_loc, s_ctx

    def block_output(nb, scores):
        rows = slice(nb * BLK, (nb + 1) * BLK)
        o = _softmax_pv(list(scores), [v_aug[key_span(nb), :], cv], sink).astype(BF16)
        for r in range(REP_B):
            o_ref[rows, r * HEAD_DIM:(r + 1) * HEAD_DIM] = o[r * BLK:(r + 1) * BLK]

    scores = [block_scores(nb) for nb in range(n_blk)]
    for nb in range(n_blk):
        block_output(nb, scores[nb])


def _lat_win(z, cache_k, cache_v, sink, cos, sin, gmlp_w, gmlp_b4, ln_g, ln_b, l):
    zc = lambda off: pl.BlockSpec((DEC_SEQ, HEAD_DIM), lambda b, g: (b, off // HEAD_DIM + g))
    cache = pl.BlockSpec((None, None, None, PAST_LEN, HEAD_DIM), lambda b, g: (b, l, g, 0, 0))
    tab = pl.BlockSpec((DEC_SEQ, HEAD_DIM), lambda b, g: (0, 0))
    qw = REP_B * HEAD_DIM
    return pl.pallas_call(
        functools.partial(_win_kernel, l),
        grid=(DEC_BATCH, KV_B),
        in_specs=[pl.BlockSpec(memory_space=pltpu.SMEM),
                  pl.BlockSpec((DEC_SEQ, qw), lambda b, g: (b, Q_B // qw + g)),
                  zc(K_B), zc(V_B), cache, cache, tab, tab,
                  pl.BlockSpec((DEC_SEQ, W_C), lambda b, g: (b, U_C // W_C)),
                  pl.BlockSpec((DEC_SEQ, W_C), lambda b, g: (b, V_C // W_C))] + _gmlp_specs(l),
        out_specs=[pl.BlockSpec((DEC_SEQ, qw), lambda b, g: (b, g)),
                   pl.BlockSpec((DEC_SEQ, W_C), lambda b, g: (b, 0))],
        out_shape=[jax.ShapeDtypeStruct((N_LAT_TOK, W_B), BF16),
                   jax.ShapeDtypeStruct((N_LAT_TOK, W_C), BF16)],
        scratch_shapes=[pltpu.VMEM((DEC_SEQ, HEAD_DIM), BF16)],
        compiler_params=_params("parallel", "arbitrary"),
        name="lat_win",
    )(sink, z, z, z, cache_k, cache_v, cos, sin, z, z, gmlp_w, gmlp_b4, ln_g, ln_b)


EPI_ROWS = 128


def _residual_epilogue(y, rows, x_ref, g_post_ref, gate_ref, xo_ref, next_norm):
    x_new = x_ref[rows, :] + _rms(y, gate_ref[...] * g_post_ref[...])
    xo_ref[rows, :] = x_new
    if next_norm is not None:
        g_ref, sc_ref, sh_ref, h_ref = next_norm
        h_ref[rows, :] = (_rms(x_new, g_ref[...] * (1.0 + sc_ref[...])) + sh_ref[...]).astype(BF16)


def _out_kernel(widths, *refs):
    o_refs = refs[:len(widths)]
    (w_ref, x_ref, g_post_ref, gate_ref, g_next_ref, sc_ref, sh_ref,
     xo_ref, h_ref) = refs[len(widths):]
    for r in range(x_ref.shape[0] // EPI_ROWS):
        rows = slice(r * EPI_ROWS, (r + 1) * EPI_ROWS)
        y = None
        off = 0
        for o_ref, wd in zip(o_refs, widths):
            part = _dot(o_ref[rows, :], w_ref[off:off + wd, :])
            y = part if y is None else y + part
            off += wd
        _residual_epilogue(y, rows, x_ref, g_post_ref, gate_ref, xo_ref,
                           (g_next_ref, sc_ref, sh_ref, h_ref))


def _out_proj(o_list, x, mod5, g_post, g_ffn_pre, w_out, l, row_fn, tm):
    m = x.shape[0]
    widths = tuple(o.shape[1] for o in o_list)
    row_spec = pl.BlockSpec((tm, D_MODEL), lambda i: (i, 0))
    return pl.pallas_call(
        functools.partial(_out_kernel, widths),
        grid=(m // tm,),
        in_specs=[pl.BlockSpec((tm, wd), lambda i: (i, 0)) for wd in widths] + [
            pl.BlockSpec((None, MIX_WIDTH, D_MODEL), lambda i: (l, 0, 0)),
            row_spec,
            _vec_spec(l),
            _mod_spec(l, 2, row_fn),
            _vec_spec(l),
            _mod_spec(l, 4, row_fn),
            _mod_spec(l, 3, row_fn),
        ],
        out_specs=[row_spec, row_spec],
        out_shape=[jax.ShapeDtypeStruct((m, D_MODEL), F32),
                   jax.ShapeDtypeStruct((m, D_MODEL), BF16)],
        compiler_params=_params("parallel"),
        name="out_proj",
    )(*o_list, w_out, x, g_post, mod5, g_ffn_pre, mod5, mod5)


FFN_TC = 2 * MXU_TILE
FFN_ROWS = 512
FFN_LAST_ROWS = 256


def _ffn_kernel(with_next, h_ref, wg_ref, wu_ref, wd_ref, x_hbm, g_post_ref, gate_ref, *refs):
    if with_next:
        g_next_ref, sc_ref, sh_ref, xo_ref, ho_ref, x_buf, x_sem = refs
        next_norm = (g_next_ref, sc_ref, sh_ref, ho_ref)
    else:
        xo_ref, x_buf, x_sem = refs
        next_norm = None
    tm = x_buf.shape[0]
    c = pl.program_id(1)
    last = pl.num_programs(1) - 1

    def x_copy():
        row0 = pl.multiple_of(pl.program_id(0) * tm, tm)
        return pltpu.make_async_copy(x_hbm.at[pl.ds(row0, tm), :], x_buf, x_sem)

    def row_blocks(n):
        return [slice(r * n, (r + 1) * n) for r in range(tm // n)]

    def chunk_products(n):
        acts = []
        for rows in row_blocks(n):
            h = h_ref[rows, :]
            acts.append((jax.nn.silu(_dot(h, wg_ref[...])) * _dot(h, wu_ref[...])).astype(BF16))
        for rows, a in zip(row_blocks(n), acts):
            yield rows, _dot(a, wd_ref[...])

    @pl.when(c == 0)
    def _():
        x_copy().start()
        for rows, y in chunk_products(FFN_ROWS):
            xo_ref[rows, :] = y

    @pl.when((c > 0) & (c < last))
    def _():
        for rows, y in chunk_products(FFN_ROWS):
            xo_ref[rows, :] += y

    @pl.when(c == last)
    def _():
        x_copy().wait()
        for rows, y in chunk_products(FFN_LAST_ROWS):
            _residual_epilogue(xo_ref[rows, :] + y, rows, x_buf, g_post_ref, gate_ref, xo_ref,
                               next_norm)


def _ffn(h, x, mod5, g_post, g_mix_pre, w_gate, w_up, w_down, l, row_fn, tm):
    m = x.shape[0]
    with_next = l + 1 < DEPTH
    row_spec = pl.BlockSpec((tm, D_MODEL), lambda i, c: (i, 0))
    up_spec = pl.BlockSpec((None, D_MODEL, FFN_TC), lambda i, c: (l, 0, c))
    in_specs = [
        row_spec, up_spec, up_spec,
        pl.BlockSpec((None, FFN_TC, D_MODEL), lambda i, c: (l, c, 0)),
        pl.BlockSpec(memory_space=pl.ANY),
        _vec_spec(l),
        _mod_spec(l, 5, row_fn),
    ]
    args = [h, w_gate, w_up, w_down, x, g_post, mod5]
    out_specs = [row_spec]
    out_shape = [jax.ShapeDtypeStruct((m, D_MODEL), F32)]
    if with_next:
        in_specs += [_vec_spec(l + 1), _mod_spec(l + 1, 1, row_fn), _mod_spec(l + 1, 0, row_fn)]
        args += [g_mix_pre, mod5, mod5]
        out_specs.append(row_spec)
        out_shape.append(jax.ShapeDtypeStruct((m, D_MODEL), BF16))
    outs = pl.pallas_call(
        functools.partial(_ffn_kernel, with_next),
        grid=(m // tm, D_FF // FFN_TC),
        in_specs=in_specs,
        out_specs=out_specs,
        out_shape=out_shape,
        scratch_shapes=[pltpu.VMEM((tm, D_MODEL), F32), pltpu.SemaphoreType.DMA(())],
        compiler_params=_params("parallel", "arbitrary"),
        name="ffn",
    )(*args)
    return outs if with_next else (outs[0], None)


def kernel(x_prompt, x_sample, cache_a_k, cache_a_v, cache_b_k, cache_b_v, c, c_ctx,
           mod_w, mod_b, norm_mix_pre, norm_mix_post, norm_ffn_pre, norm_ffn_post,
           w_in, w_out, rpb_a, sink_b, gmlp_ln_g, gmlp_ln_b, gmlp_w, gmlp_b,
           w_gate, w_up, w_down):
    cond = jnp.zeros((MOD_ROWS, D_MODEL), F32)
    cond = cond.at[:DEC_BATCH].set(c).at[CTX_MOD_ROW].set(c_ctx)
    mod5 = _modulation(cond, mod_w, mod_b).reshape(DEPTH, MOD_ROWS, N_MOD, 1, D_MODEL)

    vec = lambda a: a.reshape(DEPTH, 1, -1)
    g_mix_pre, g_mix_post = vec(norm_mix_pre), vec(norm_mix_post)
    g_ffn_pre, g_ffn_post = vec(norm_ffn_pre), vec(norm_ffn_post)
    ln_g, ln_b = vec(gmlp_ln_g), vec(gmlp_ln_b)
    gmlp_b4 = gmlp_b.reshape(DEPTH, G_C, CHUNK, 1)
    sink = sink_b.reshape(DEPTH * H_B)
    w_in_b, w_out_b = w_in.astype(BF16), w_out.astype(BF16)
    w_gate_b, w_up_b, w_down_b = w_gate.astype(BF16), w_up.astype(BF16), w_down.astype(BF16)
    bias = _na_bias(rpb_a)
    cos, sin = _rope_tables()

    tm_wide, tm_row = 1024, 512
    ctx_row = lambda i: CTX_MOD_ROW
    lat_row = lambda tm: (lambda i: (i * tm) // DEC_SEQ)

    xp = x_prompt.reshape(N_CTX_TOK, D_MODEL)
    xs = x_sample.reshape(N_LAT_TOK, D_MODEL)
    hp = _norm_mod(xp, mod5, g_mix_pre, 0, ctx_row, tm_row)
    hs = _norm_mod(xs, mod5, g_mix_pre, 0, lat_row(tm_row), tm_row)
    kv_all = None
    for l in range(DEPTH):
        o, *kv_all = _ctx_mix(hp, w_in_b, sink, gmlp_w, gmlp_b4, ln_g, ln_b, kv_all, l)
        xp, hp = _out_proj([o], xp, mod5, g_mix_post, g_ffn_pre, w_out_b, l, ctx_row, tm_row)
        xp, hp = _ffn(hp, xp, mod5, g_ffn_post, g_mix_pre, w_gate_b, w_up_b, w_down_b, l,
                      ctx_row, tm_wide)
        z = _in_proj(hs, w_in_b, l, tm_wide)
        o_a = _lat_na(z, cache_a_k, cache_a_v, bias, l)
        o_b, o_c = _lat_win(z, cache_b_k, cache_b_v, sink, cos, sin, gmlp_w, gmlp_b4, ln_g, ln_b, l)
        xs, hs = _out_proj([o_a, o_b, o_c], xs, mod5, g_mix_post, g_ffn_pre, w_out_b, l,
                           lat_row(tm_row), tm_row)
        xs, hs = _ffn(hs, xs, mod5, g_ffn_post, g_mix_pre, w_gate_b, w_up_b, w_down_b, l,
                      lat_row(tm_wide), tm_wide)

    return (xp.reshape(BATCH, SEQ, D_MODEL), xs.reshape(DEC_BATCH, DEC_SEQ, D_MODEL), *kv_all)
```
